```python
import math
import jax, jax.numpy as jnp
from jax import lax
import numpy as np

D_MODEL = 2048
BATCH = 1
SEQ = 16384
DEPTH = 4

GRID_W = 64
CTX_LEN = 256
ROT_DIM = 64
ROPE_THETA = 10000.0
EPS = 1e-6
BLOCK_Q = 128

MLA_HEADS = 6
MLA_NOPE = 128
MLA_ROPE = ROT_DIM
MLA_V = 128
MLA_Q_RANK = 512
MLA_KV_RANK = 256
NA_HEADS = 5
NA_DIM = 128
WIN_H = 8
WIN_W = 16
DIFF_HEADS = 5
DIFF_QK = ROT_DIM
DIFF_V = 2 * DIFF_QK

MLA_W = MLA_HEADS * MLA_V
NA_W = NA_HEADS * NA_DIM
DIFF_W = DIFF_HEADS * DIFF_V
MIX = MLA_W + NA_W + DIFF_W

IN_SIZES = (MLA_Q_RANK, MLA_KV_RANK, MLA_ROPE,
            NA_W, NA_W, NA_W,
            DIFF_HEADS * 2 * DIFF_QK, DIFF_HEADS * 2 * DIFF_QK, DIFF_W,
            MIX)
IN_WIDTH = int(sum(IN_SIZES))
IN_SPLITS = tuple(int(s) for s in np.cumsum(IN_SIZES)[:-1])

kernel_name = 'hybrid_mla_natten_diff_dit'


def rms_norm(x, g):
    xf = x.astype(jnp.float32)
    y = xf * lax.rsqrt(jnp.mean(xf * xf, axis=-1, keepdims=True) + EPS)
    return (y * g.astype(jnp.float32)).astype(x.dtype)


def axial_rope_tables(n, dim, dtype):
    t = jnp.arange(n)
    row = (t // GRID_W).astype(jnp.float32)
    col = (t % GRID_W).astype(jnp.float32)
    half = dim // 2
    inv = 1.0 / (ROPE_THETA ** (jnp.arange(0, half, 2, dtype=jnp.float32) / half))
    ang_r = row[:, None] * inv[None]
    ang_c = col[:, None] * inv[None]
    ang = jnp.concatenate([ang_r, ang_r, ang_c, ang_c], axis=-1)
    return jnp.cos(ang).astype(dtype), jnp.sin(ang).astype(dtype)


def apply_axial_rope(t, cos, sin):
    q = t.shape[-1] // 4
    a1, a2, b1, b2 = t[..., :q], t[..., q:2 * q], t[..., 2 * q:3 * q], t[..., 3 * q:]
    rot = jnp.concatenate([-a2, a1, -b2, b1], axis=-1)
    return t * cos + rot * sin


def split_heads(t, h):
    return t.reshape(t.shape[:2] + (h, t.shape[-1] // h))


def sweep_query_blocks(fn, *qs):
    B, n = qs[0].shape[:2]
    nb = n // BLOCK_Q
    blocks = tuple(jnp.moveaxis(q.reshape((B, nb, BLOCK_Q) + q.shape[2:]), 1, 0) for q in qs)
    out = lax.map(lambda a: fn(*a), blocks)
    return jnp.moveaxis(out, 0, 1).reshape((B, n) + out.shape[3:])


def mla_queries(cq, q_norm, w_uq):
    q = split_heads(rms_norm(cq, q_norm) @ w_uq, MLA_HEADS)
    return q[..., :MLA_NOPE], q[..., MLA_NOPE:]


def mla_keys(ckv, kv_norm, w_ukv):
    kv = split_heads(rms_norm(ckv, kv_norm) @ w_ukv, MLA_HEADS)
    return kv[..., :MLA_NOPE], kv[..., MLA_NOPE:]


def mla_attend(qn, qr, kn, kr, v):
    s = jnp.einsum('bqhd,bkhd->bhqk', qn, kn) + jnp.einsum('bqhr,bkr->bhqk', qr, kr)
    p = jax.nn.softmax(s.astype(jnp.float32) * (MLA_NOPE + MLA_ROPE) ** -0.5, axis=-1)
    return jnp.einsum('bhqk,bkhd->bqhd', p.astype(v.dtype), v)


def dense_attend(q, k, v):
    s = jnp.einsum('bqhd,bkhd->bhqk', q, k).astype(jnp.float32) * q.shape[-1] ** -0.5
    p = jax.nn.softmax(s, axis=-1)
    return jnp.einsum('bhqk,bkhd->bqhd', p.astype(v.dtype), v)


def na_latent(q, k, v, kc, vc, rpb):
    B, n, H, d = q.shape
    rows = n // GRID_W
    kh = min(WIN_H, rows)
    kw = WIN_W
    qg = q.reshape(B, rows, GRID_W, H, d)
    kg = k.reshape(B, rows, GRID_W, H, d)
    vg = v.reshape(B, rows, GRID_W, H, d)
    cols = jnp.arange(GRID_W)
    col_idx = jnp.clip(cols - kw // 2, 0, GRID_W - kw)[:, None] + jnp.arange(kw)[None]
    col_off = col_idx - cols[:, None] + (WIN_W - 1)
    scale = d ** -0.5

    def row_block(r):
        rs = jnp.clip(r - kh // 2, 0, rows - kh)
        q_r = lax.dynamic_index_in_dim(qg, r, axis=1, keepdims=False)
        k_win = lax.dynamic_slice_in_dim(kg, rs, kh, axis=1)[:, :, col_idx]
        v_win = lax.dynamic_slice_in_dim(vg, rs, kh, axis=1)[:, :, col_idx]
        row_off = rs + jnp.arange(kh) - r + (WIN_H - 1)
        bias = rpb[:, row_off][:, :, col_off].transpose(0, 2, 1, 3)
        s_win = (jnp.einsum('bqhd,baqkhd->bhqak', q_r, k_win).astype(jnp.float32) * scale
                 + bias[None].astype(jnp.float32))
        s_ctx = jnp.einsum('bqhd,bchd->bhqc', q_r, kc).astype(jnp.float32) * scale
        p = jax.nn.softmax(jnp.concatenate([s_win.reshape(B, H, GRID_W, kh * kw), s_ctx], axis=-1), axis=-1)
        p = p.astype(v.dtype)
        p_win = p[..., :kh * kw].reshape(B, H, GRID_W, kh, kw)
        p_ctx = p[..., kh * kw:]
        return (jnp.einsum('bhqak,baqkhd->bqhd', p_win, v_win)
                + jnp.einsum('bhqc,bchd->bqhd', p_ctx, vc))

    out = lax.map(row_block, jnp.arange(rows))
    return jnp.moveaxis(out, 0, 1).reshape(B, n, H, d)


def diff_attend(q1, q2, k1, k2, v, lam):
    scale = DIFF_QK ** -0.5
    s1 = jnp.einsum('bqhd,bkhd->bhqk', q1, k1).astype(jnp.float32) * scale
    s2 = jnp.einsum('bqhd,bkhd->bhqk', q2, k2).astype(jnp.float32) * scale
    a = jax.nn.softmax(s1, axis=-1) - lam * jax.nn.softmax(s2, axis=-1)
    return jnp.einsum('bhqk,bkhd->bqhd', a.astype(v.dtype), v)


def diff_post(o, subln, lam_init):
    return rms_norm(o, subln) * (1.0 - lam_init)


def mix_out(o_mla, o_na, o_diff, g, w_out):
    B, n = g.shape[:2]
    y = jnp.concatenate([o_mla.reshape(B, n, MLA_W), o_na.reshape(B, n, NA_W),
                         o_diff.reshape(B, n, DIFF_W)], axis=-1) * jax.nn.silu(g)
    return y @ w_out


def setup_inputs(seed: int = 0) -> dict:
    key = jax.random.key(seed)
    ks = jax.random.split(key, 20)

    def nrm(k, shape, s):
        return jax.random.normal(k, shape, jnp.float32) * s

    L, D = DEPTH, D_MODEL
    return {
        'x': nrm(ks[0], (BATCH, SEQ, D), 1.0),
        'c': nrm(ks[1], (BATCH, D), 1.0),
        'ctx': nrm(ks[2], (BATCH, CTX_LEN, D), 1.0),
        'c_ctx': nrm(ks[3], (D,), 1.0),
        'w_ada': nrm(ks[4], (L, D, 3 * D), 0.5 * D ** -0.5),
        'b_ada': nrm(ks[5], (L, 3 * D), 0.01),
        'norm_pre': 1.0 + nrm(ks[6], (L, D), 0.02),
        'norm_post': 1.0 + nrm(ks[7], (L, D), 0.02),
        'w_in': nrm(ks[8], (L, D, IN_WIDTH), D ** -0.5),
        'mla_q_norm': 1.0 + nrm(ks[9], (L, MLA_Q_RANK), 0.02),
        'mla_kv_norm': 1.0 + nrm(ks[10], (L, MLA_KV_RANK), 0.02),
        'w_uq': nrm(ks[11], (L, MLA_Q_RANK, MLA_HEADS * (MLA_NOPE + MLA_ROPE)), MLA_Q_RANK ** -0.5),
        'w_ukv': nrm(ks[12], (L, MLA_KV_RANK, MLA_HEADS * (MLA_NOPE + MLA_V)), MLA_KV_RANK ** -0.5),
        'na_rpb': nrm(ks[13], (L, NA_HEADS, 2 * WIN_H - 1, 2 * WIN_W - 1), 0.1),
        'diff_lq1': nrm(ks[14], (L, DIFF_QK), 0.1),
        'diff_lk1': nrm(ks[15], (L, DIFF_QK), 0.1),
        'diff_lq2': nrm(ks[16], (L, DIFF_QK), 0.1),
        'diff_lk2': nrm(ks[17], (L, DIFF_QK), 0.1),
        'diff_subln': 1.0 + nrm(ks[18], (L, DIFF_V), 0.02),
        'w_out': nrm(ks[19], (L, MIX, D), MIX ** -0.5),
    }


def reference(x, c, ctx, c_ctx, w_ada, b_ada, norm_pre, norm_post, w_in, mla_q_norm, mla_kv_norm,
              w_uq, w_ukv, na_rpb, diff_lq1, diff_lk1, diff_lq2, diff_lk2, diff_subln, w_out):
    B, n, _ = x.shape
    C = ctx.shape[1]
    cos, sin = axial_rope_tables(n, ROT_DIM, x.dtype)
    cos_h, sin_h = cos[:, None, :], sin[:, None, :]
    cos_d, sin_d = cos[:, None, None, :], sin[:, None, None, :]
    hc = ctx
    for i in range(DEPTH):
        last = i == DEPTH - 1
        lam_init = 0.8 - 0.6 * math.exp(-0.3 * i)
        lam = (jnp.exp(jnp.sum(diff_lq1[i].astype(jnp.float32) * diff_lk1[i].astype(jnp.float32)))
               - jnp.exp(jnp.sum(diff_lq2[i].astype(jnp.float32) * diff_lk2[i].astype(jnp.float32)))
               + lam_init)

        shift, scale, gate = jnp.split(jax.nn.silu(c) @ w_ada[i] + b_ada[i], 3, axis=-1)
        shift_c, scale_c, gate_c = jnp.split(jax.nn.silu(c_ctx) @ w_ada[i] + b_ada[i], 3, axis=-1)
        h = rms_norm(x, norm_pre[i]) * (1.0 + scale[:, None]) + shift[:, None]
        hcn = rms_norm(hc, norm_pre[i]) * (1.0 + scale_c) + shift_c

        cq, ckv, kr, na_q, na_k, na_v, dq, dk, dv, g = jnp.split(h @ w_in[i], IN_SPLITS, axis=-1)
        ccq, cckv, ckr, cna_q, cna_k, cna_v, cdq, cdk, cdv, cg = jnp.split(hcn @ w_in[i], IN_SPLITS, axis=-1)

        m_kn_c, m_v_c = mla_keys(cckv, mla_kv_norm[i], w_ukv[i])
        na_kc, na_vc = split_heads(cna_k, NA_HEADS), split_heads(cna_v, NA_HEADS)
        d_kc = cdk.reshape(B, C, DIFF_HEADS, 2, DIFF_QK)
        d_vc = split_heads(cdv, DIFF_HEADS)

        m_qn, m_qr = mla_queries(cq, mla_q_norm[i], w_uq[i])
        m_qr = apply_axial_rope(m_qr, cos_h, sin_h)
        m_kn, m_v = mla_keys(ckv, mla_kv_norm[i], w_ukv[i])
        kn_all = jnp.concatenate([m_kn_c, m_kn], axis=1)
        kr_all = jnp.concatenate([ckr, apply_axial_rope(kr, cos, sin)], axis=1)
        v_all = jnp.concatenate([m_v_c, m_v], axis=1)
        o_mla = sweep_query_blocks(lambda qn, qr: mla_attend(qn, qr, kn_all, kr_all, v_all), m_qn, m_qr)

        o_na = na_latent(split_heads(na_q, NA_HEADS), split_heads(na_k, NA_HEADS),
                         split_heads(na_v, NA_HEADS), na_kc, na_vc, na_rpb[i])

        d_q = apply_axial_rope(dq.reshape(B, n, DIFF_HEADS, 2, DIFF_QK), cos_d, sin_d)
        d_k = apply_axial_rope(dk.reshape(B, n, DIFF_HEADS, 2, DIFF_QK), cos_d, sin_d)
        k1_all = jnp.concatenate([d_kc[..., 0, :], d_k[..., 0, :]], axis=1)
        k2_all = jnp.concatenate([d_kc[..., 1, :], d_k[..., 1, :]], axis=1)
        dv_all = jnp.concatenate([d_vc, split_heads(dv, DIFF_HEADS)], axis=1)
        o_diff = sweep_query_blocks(lambda q1, q2: diff_attend(q1, q2, k1_all, k2_all, dv_all, lam),
                                    d_q[..., 0, :], d_q[..., 1, :])
        o_diff = diff_post(o_diff, diff_subln[i], lam_init)

        y = mix_out(o_mla, o_na, o_diff, g, w_out[i])
        new_x = x + gate[:, None] * rms_norm(y, norm_post[i])

        if not last:
            cm_qn, cm_qr = mla_queries(ccq, mla_q_norm[i], w_uq[i])
            oc_mla = mla_attend(cm_qn, cm_qr, m_kn_c, ckr, m_v_c)
            oc_na = dense_attend(split_heads(cna_q, NA_HEADS), na_kc, na_vc)
            cd_q = cdq.reshape(B, C, DIFF_HEADS, 2, DIFF_QK)
            oc_diff = diff_post(diff_attend(cd_q[..., 0, :], cd_q[..., 1, :], d_kc[..., 0, :],
                                            d_kc[..., 1, :], d_vc, lam), diff_subln[i], lam_init)
            yc = mix_out(oc_mla, oc_na, oc_diff, cg, w_out[i])
            hc = hc + gate_c * rms_norm(yc, norm_post[i])
        x = new_x
    return x
```

```python
import functools
import math

import jax
import jax.numpy as jnp
from jax import lax
from jax.experimental import pallas as pl
from jax.experimental.pallas import tpu as pltpu

GRID_W = 64
ROT_DIM = 64
ROPE_THETA = 10000.0
EPS = 1e-6
MLA_HEADS, MLA_NOPE, MLA_ROPE, MLA_V = 6, 128, 64, 128
MLA_Q_RANK, MLA_KV_RANK = 512, 256
NA_HEADS, NA_DIM, WIN_H, WIN_W = 5, 128, 8, 16
DIFF_HEADS, DIFF_QK, DIFF_V = 5, 64, 128
MLA_W = MLA_HEADS * MLA_V
NA_W = NA_HEADS * NA_DIM
DIFF_W = DIFF_HEADS * DIFF_V
MIX = MLA_W + NA_W + DIFF_W

LANES = 128
LOG2E = 1.4426950408889634
NEG = -1e30
VMEM_LIMIT = 56 * 1024 * 1024

F32 = jnp.float32
BF16 = jnp.bfloat16

_NT = (((1,), (1,)), ((), ()))
_TN = (((0,), (0,)), ((), ()))


def _cparams(n_axes):
    return pltpu.CompilerParams(
        dimension_semantics=("arbitrary",) * n_axes, vmem_limit_bytes=VMEM_LIMIT)


def _dot(a, b):
    return jnp.dot(a, b, preferred_element_type=F32)


def _dot_nt(a, b):
    return lax.dot_general(a, b, _NT, preferred_element_type=F32)


def _rms(x, g):
    return x * lax.rsqrt(jnp.mean(x * x, axis=-1, keepdims=True) + EPS) * g


def _mod_kernel(c_ref, w_ref, b_ref, o_ref):
    c = c_ref[...]
    a = c * (1.0 / (1.0 + jnp.exp(-c)))
    o_ref[0] = _dot(a.astype(BF16), w_ref[0].astype(BF16)) + b_ref[0]


def _modulation(cc, w_ada, b_ada):
    L, D, N = w_ada.shape
    tn = 1536
    return pl.pallas_call(
        _mod_kernel,
        out_shape=jax.ShapeDtypeStruct((L, 8, N), F32),
        grid=(L, N // tn),
        in_specs=[pl.BlockSpec((8, D), lambda l, j: (0, 0)),
                  pl.BlockSpec((1, D, tn), lambda l, j: (l, 0, j)),
                  pl.BlockSpec((1, 1, tn), lambda l, j: (l, 0, j))],
        out_specs=pl.BlockSpec((1, 8, tn), lambda l, j: (l, 0, j)),
        compiler_params=_cparams(2), name="adaln_mod",
    )(cc, w_ada, b_ada.reshape(L, 1, N))


def _prologue_kernel(x_ref, g_ref, sc_ref, sh_ref, h_ref):
    y = _rms(x_ref[...], g_ref[...])
    h_ref[...] = (y * (1.0 + sc_ref[...]) + sh_ref[...]).astype(BF16)


def _prologue(x, g, sc, sh, bm):
    T, D = x.shape
    vec = pl.BlockSpec((1, D), lambda i: (0, 0))
    return pl.pallas_call(
        _prologue_kernel,
        out_shape=jax.ShapeDtypeStruct((T, D), BF16),
        grid=(T // bm,),
        in_specs=[pl.BlockSpec((bm, D), lambda i: (i, 0)), vec, vec, vec],
        out_specs=pl.BlockSpec((bm, D), lambda i: (i, 0)),
        compiler_params=_cparams(1), name="prenorm_mod",
    )(x, g, sc, sh)


def _proj_raw_kernel(h_ref, w_ref, o_ref):
    o_ref[...] = _dot(h_ref[...], w_ref[...])


def _proj_na_kernel(h_ref, w_ref, q_ref, k_ref, v_ref, *, qscale):
    r = _dot(h_ref[...], w_ref[...])
    q_ref[...] = (r[:, :NA_W] * qscale).astype(BF16)
    k_ref[...] = r[:, NA_W:2 * NA_W].astype(BF16)
    v_ref[...] = r[:, 2 * NA_W:].astype(BF16)


def _proj_dqk_kernel(h_ref, w_ref, cs_ref, q_ref, k_ref, *, qscale):
    r = _dot(h_ref[...], w_ref[...])
    cos = jnp.concatenate([cs_ref[:, :LANES]] * DIFF_HEADS, axis=1)
    sin = jnp.concatenate([cs_ref[:, LANES:]] * DIFF_HEADS, axis=1)
    W = DIFF_W
    q_ref[...] = ((r[:, :W] * cos + r[:, W:2 * W] * sin) * qscale).astype(BF16)
    k_ref[...] = (r[:, 2 * W:3 * W] * cos + r[:, 3 * W:] * sin).astype(BF16)


def _proj_vt_kernel(h_ref, w_ref, vt_ref):
    vt_ref[0] = _dot(h_ref[...], w_ref[...]).T.astype(BF16)


def _proj_gate_kernel(h_ref, w_ref, o_ref):
    g = _dot(h_ref[...], w_ref[...])
    o_ref[...] = (g * (1.0 / (1.0 + jnp.exp(-g)))).astype(BF16)


def _row_proj(kernel, h, w, extra, outs, bm, name):
    T, D = h.shape
    N = w.shape[1]
    in_specs = [pl.BlockSpec((bm, D), lambda i: (i, 0)), pl.BlockSpec((D, N), lambda i: (0, 0))]
    in_specs += [pl.BlockSpec((bm, e.shape[1]), lambda i: (i, 0)) for e in extra]
    out_shape, out_specs = [], []
    for cols, dt, transposed in outs:
        if transposed:
            out_shape.append(jax.ShapeDtypeStruct((T // bm, cols, bm), dt))
            out_specs.append(pl.BlockSpec((1, cols, bm), lambda i: (i, 0, 0)))
        else:
            out_shape.append(jax.ShapeDtypeStruct((T, cols), dt))
            out_specs.append(pl.BlockSpec((bm, cols), lambda i: (i, 0)))
    single = len(outs) == 1
    return pl.pallas_call(
        kernel,
        out_shape=out_shape[0] if single else tuple(out_shape),
        grid=(T // bm,),
        in_specs=in_specs,
        out_specs=out_specs[0] if single else tuple(out_specs),
        compiler_params=_cparams(1), name=name,
    )(h, w, *extra)


def _mla_proj_kernel(raw_ref, cs_ref, qn_ref, kvn_ref, wq_ref, wkv_ref, qm_ref, km_ref, vt_ref, *, qscale):
    cos = cs_ref[:, :LANES]
    sin = cs_ref[:, LANES:]
    cq = raw_ref[:, :MLA_Q_RANK]
    ckv = raw_ref[:, MLA_Q_RANK:MLA_Q_RANK + MLA_KV_RANK]
    o = MLA_Q_RANK + MLA_KV_RANK
    kr = raw_ref[:, o:o + LANES]
    kr_rot = raw_ref[:, o + LANES:o + 2 * LANES]
    qa = _dot(_rms(cq, qn_ref[...]).astype(BF16), wq_ref[...])
    hw = 2 * LANES
    for h in range(MLA_HEADS):
        rope = qa[:, h * hw + LANES:(h + 1) * hw] * cos + qa[:, MLA_HEADS * hw + h * LANES:MLA_HEADS * hw + (h + 1) * LANES] * sin
        qm_ref[:, h * hw:h * hw + LANES] = (qa[:, h * hw:h * hw + LANES] * qscale).astype(BF16)
        qm_ref[:, h * hw + LANES:(h + 1) * hw] = (rope * qscale).astype(BF16)
    kva = _dot(_rms(ckv, kvn_ref[...]).astype(BF16), wkv_ref[...])
    k_rope = (kr * cos + kr_rot * sin).astype(BF16)
    for h in range(MLA_HEADS):
        km_ref[:, h * hw:h * hw + LANES] = kva[:, h * LANES:(h + 1) * LANES].astype(BF16)
        km_ref[:, h * hw + LANES:(h + 1) * hw] = k_rope
    vt_ref[0] = kva[:, MLA_HEADS * LANES:].T.astype(BF16)


def _mla_proj(raw, cs, qn, kvn, wq, wkv, bm, qscale):
    T = raw.shape[0]
    row = lambda c: pl.BlockSpec((bm, c), lambda i: (i, 0))
    full = lambda a: pl.BlockSpec(a.shape, lambda i: (0, 0))
    hw = 2 * LANES
    return pl.pallas_call(
        functools.partial(_mla_proj_kernel, qscale=qscale),
        out_shape=(jax.ShapeDtypeStruct((T, MLA_HEADS * hw), BF16),
                   jax.ShapeDtypeStruct((T, MLA_HEADS * hw), BF16),
                   jax.ShapeDtypeStruct((T // bm, MLA_W, bm), BF16)),
        grid=(T // bm,),
        in_specs=[row(raw.shape[1]), row(cs.shape[1]), full(qn), full(kvn), full(wq), full(wkv)],
        out_specs=(row(MLA_HEADS * hw), row(MLA_HEADS * hw),
                   pl.BlockSpec((1, MLA_W, bm), lambda i: (i, 0, 0))),
        compiler_params=_cparams(1), name="mla_proj",
    )(raw, cs, qn, kvn, wq, wkv)


def _softmax_first(s):
    m = jnp.max(s, axis=0, keepdims=True)
    p = jnp.exp2(s - m)
    return m, jnp.sum(p, axis=0, keepdims=True), p


def _softmax_next(s, m_prev):
    m = jnp.maximum(m_prev, jnp.max(s, axis=0, keepdims=True))
    p = jnp.exp2(s - m)
    return m, jnp.exp2(m_prev - m), jnp.sum(p, axis=0, keepdims=True), p


def _flash_kernel(q_ref, kc_ref, vtc_ref, *rest, n_chunks, bkc):
    if n_chunks:
        kl_ref, vtl_ref, o_ref, acc_ref, m_ref, l_ref = rest
    else:
        o_ref, acc_ref, m_ref, l_ref = rest
    q = q_ref[...]
    s = _dot_nt(kc_ref[...], q)
    m, l, p = _softmax_first(s)
    m_ref[...] = m
    l_ref[...] = l
    acc_ref[...] = _dot(vtc_ref[0], p.astype(BF16))

    def body(c, carry):
        k = kl_ref[pl.ds(pl.multiple_of(c * bkc, bkc), bkc), :]
        s = _dot_nt(k, q)
        m, alpha, ps, p = _softmax_next(s, m_ref[...])
        m_ref[...] = m
        l_ref[...] = alpha * l_ref[...] + ps
        acc_ref[...] = alpha * acc_ref[...] + _dot(vtl_ref[c], p.astype(BF16))
        return carry

    if n_chunks:
        lax.fori_loop(0, n_chunks, body, 0)
    o_ref[...] = (acc_ref[...] * (1.0 / l_ref[...])).T.astype(o_ref.dtype)


def _flash(q, kc, vtc, kl, vtl, heads, dk, dv, bq):
    Tq = q.shape[0]
    C = kc.shape[0]
    in_specs = [pl.BlockSpec((bq, dk), lambda h, i: (i, h)),
                pl.BlockSpec((C, dk), lambda h, i: (0, h)),
                pl.BlockSpec((1, dv, C), lambda h, i: (0, h, 0))]
    args = [q, kc, vtc]
    n_chunks, bkc = 0, 0
    if kl is not None:
        n_chunks, _, bkc = vtl.shape
        in_specs += [pl.BlockSpec((kl.shape[0], dk), lambda h, i: (0, h)),
                     pl.BlockSpec((n_chunks, dv, bkc), lambda h, i: (0, h, 0))]
        args += [kl, vtl]
    return pl.pallas_call(
        functools.partial(_flash_kernel, n_chunks=n_chunks, bkc=bkc),
        out_shape=jax.ShapeDtypeStruct((Tq, heads * dv), BF16),
        grid=(heads, Tq // bq),
        in_specs=in_specs,
        out_specs=pl.BlockSpec((bq, dv), lambda h, i: (i, h)),
        scratch_shapes=[pltpu.VMEM((dv, bq), F32), pltpu.VMEM((1, bq), F32), pltpu.VMEM((1, bq), F32)],
        compiler_params=_cparams(2), name="flash_attn",
    )(*args)


def _diff_kernel(q_ref, kc_ref, vtc_ref, *rest, n_chunks, bkc, lam_init):
    if n_chunks:
        kl_ref, vtl_ref, lam_ref, sub_ref, o_ref, acc1, acc2, m1, l1, m2, l2 = rest
    else:
        lam_ref, sub_ref, o_ref, acc1, acc2, m1, l1, m2, l2 = rest
    q = q_ref[...]
    lane = lax.broadcasted_iota(jnp.int32, q.shape, 1)
    zero = jnp.zeros_like(q)
    qs = (jnp.where(lane < DIFF_QK, q, zero), jnp.where(lane >= DIFF_QK, q, zero))
    accs, ms, ls = (acc1, acc2), (m1, m2), (l1, l2)

    kc = kc_ref[...]
    vtc = vtc_ref[0]
    for a in range(2):
        s = _dot_nt(kc, qs[a])
        m, l, p = _softmax_first(s)
        ms[a][...] = m
        ls[a][...] = l
        accs[a][...] = _dot(vtc, p.astype(BF16))

    def body(c, carry):
        k = kl_ref[pl.ds(pl.multiple_of(c * bkc, bkc), bkc), :]
        vt = vtl_ref[c]
        for a in range(2):
            s = _dot_nt(k, qs[a])
            m, alpha, ps, p = _softmax_next(s, ms[a][...])
            ms[a][...] = m
            ls[a][...] = alpha * ls[a][...] + ps
            accs[a][...] = alpha * accs[a][...] + _dot(vt, p.astype(BF16))
        return carry

    if n_chunks:
        lax.fori_loop(0, n_chunks, body, 0)

    lv = lam_ref[...]
    lam = (jnp.exp(jnp.sum(lv[0:1] * lv[1:2], axis=1, keepdims=True))
           - jnp.exp(jnp.sum(lv[2:3] * lv[3:4], axis=1, keepdims=True)) + lam_init)
    o = acc1[...] * (1.0 / l1[...]) - lam * (acc2[...] * (1.0 / l2[...]))
    o = o * lax.rsqrt(jnp.mean(o * o, axis=0, keepdims=True) + EPS) * sub_ref[...] * (1.0 - lam_init)
    o_ref[...] = o.T.astype(o_ref.dtype)


def _diff(q, kc, vtc, kl, vtl, lam_vec, subln, lam_init, bq):
    Tq = q.shape[0]
    C = kc.shape[0]
    dk, dv = 2 * DIFF_QK, DIFF_V
    in_specs = [pl.BlockSpec((bq, dk), lambda h, i: (i, h)),
                pl.BlockSpec((C, dk), lambda h, i: (0, h)),
                pl.BlockSpec((1, dv, C), lambda h, i: (0, h, 0))]
    args = [q, kc, vtc]
    n_chunks, bkc = 0, 0
    if kl is not None:
        n_chunks, _, bkc = vtl.shape
        in_specs += [pl.BlockSpec((kl.shape[0], dk), lambda h, i: (0, h)),
                     pl.BlockSpec((n_chunks, dv, bkc), lambda h, i: (0, h, 0))]
        args += [kl, vtl]
    in_specs += [pl.BlockSpec((4, DIFF_QK), lambda h, i: (0, 0)), pl.BlockSpec((dv, 1), lambda h, i: (0, 0))]
    args += [lam_vec, subln]
    vec = pltpu.VMEM((1, bq), F32)
    return pl.pallas_call(
        functools.partial(_diff_kernel, n_chunks=n_chunks, bkc=bkc, lam_init=lam_init),
        out_shape=jax.ShapeDtypeStruct((Tq, DIFF_HEADS * dv), BF16),
        grid=(DIFF_HEADS, Tq // bq),
        in_specs=in_specs,
        out_specs=pl.BlockSpec((bq, dv), lambda h, i: (i, h)),
        scratch_shapes=[pltpu.VMEM((dv, bq), F32), pltpu.VMEM((dv, bq), F32), vec, vec, vec, vec],
        compiler_params=_cparams(2), name="diff_attn",
    )(*args)


_NA_WIN = (WIN_H + 1) * GRID_W
_NA_VARIANTS = ((3, (0, 7), (1, 8)),
                (7, (0, 7), (0, 7)),
                (5, (0, 7), (0, 7)),
                (2, (1, 8), (1, 8)),
                (0, (1, 8), (1, 8)))


def _na_build_bias(rev_ref, bias_ref):
    shp = (GRID_W, LANES)
    kc = lax.broadcasted_iota(jnp.int32, shp, 0)
    lane = lax.broadcasted_iota(jnp.int32, shp, 1)
    left = lane < GRID_W
    c = jnp.where(left, lane, lane - GRID_W)
    cs = jnp.clip(c - WIN_W // 2, 0, GRID_W - WIN_W)
    col_ok = (kc >= cs) & (kc < cs + WIN_W)
    neg = jnp.full(shp, NEG, F32)

    def toeplitz(i, shift):
        row = jnp.broadcast_to(rev_ref[i:i + 1, :], shp)
        return pltpu.roll(row, shift, 1, stride=1, stride_axis=0)

    n_off = 2 * WIN_H - 1
    pair = {}
    for i in range(1, n_off):
        t = jnp.where(left, toeplitz(i, LANES - (WIN_W - 1)), toeplitz(i - 1, GRID_W - (WIN_W - 1)))
        pair[i] = jnp.where(col_ok, t * LOG2E, neg)
    for v, (i0, (llo, lhi), (rlo, rhi)) in enumerate(_NA_VARIANTS):
        for t in range(WIN_H + 1):
            lok, rok = llo <= t <= lhi, rlo <= t <= rhi
            if lok and rok:
                tile = pair[i0 + t]
            elif lok:
                tile = jnp.where(left, pair[i0 + t], neg)
            elif rok:
                tile = jnp.where(left, neg, pair[i0 + t])
            else:
                tile = neg
            bias_ref[v, t * GRID_W:(t + 1) * GRID_W, :] = tile


def _na_pair(q, kwin, vwin, kc, vc, bias):
    s_w = _dot_nt(kwin, q) + bias
    s_c = _dot_nt(kc, q)
    m = jnp.maximum(jnp.max(s_w, axis=0, keepdims=True), jnp.max(s_c, axis=0, keepdims=True))
    p_w = jnp.exp2(s_w - m)
    p_c = jnp.exp2(s_c - m)
    inv = 1.0 / (jnp.sum(p_w, axis=0, keepdims=True) + jnp.sum(p_c, axis=0, keepdims=True))
    o = _dot((p_w * inv).T.astype(BF16), vwin) + _dot((p_c * inv).T.astype(BF16), vc)
    return o


def _na_kernel(q_ref, k_ref, v_ref, kc_ref, vc_ref, rev_ref, o_ref, bias_ref, *, rows):
    _na_build_bias(rev_ref.at[0], bias_ref)
    kc = kc_ref[...]
    vc = vc_ref[...]
    pq = 2 * GRID_W

    def run(q0, w0, bias):
        o = _na_pair(q_ref[pl.ds(q0, pq), :], k_ref[pl.ds(w0, _NA_WIN), :], v_ref[pl.ds(w0, _NA_WIN), :],
                     kc, vc, bias)
        o_ref[pl.ds(q0, pq), :] = o.astype(o_ref.dtype)

    run(0, 0, bias_ref[1])
    run(pq, 0, bias_ref[2])
    run((rows - 4) * GRID_W, (rows - 9) * GRID_W, bias_ref[3])
    run((rows - 2) * GRID_W, (rows - 9) * GRID_W, bias_ref[4])

    def body(p, carry):
        q0 = pl.multiple_of(p * pq, pq)
        w0 = pl.multiple_of(p * pq - (WIN_H // 2) * GRID_W, pq)
        run(q0, w0, bias_ref[0])
        return carry

    lax.fori_loop(2, rows // 2 - 2, body, 0)


def _na(q, k, v, kc, vc, rev):
    n = q.shape[0]
    C = kc.shape[0]
    rows = n // GRID_W
    assert rows >= 2 * WIN_H and rows % 2 == 0
    d = NA_DIM
    tok = lambda t: pl.BlockSpec((t, d), lambda h: (0, h))
    return pl.pallas_call(
        functools.partial(_na_kernel, rows=rows),
        out_shape=jax.ShapeDtypeStruct((n, NA_W), BF16),
        grid=(NA_HEADS,),
        in_specs=[tok(n), tok(n), tok(n), tok(C), tok(C),
                  pl.BlockSpec((1, 16, LANES), lambda h: (h, 0, 0))],
        out_specs=tok(n),
        scratch_shapes=[pltpu.VMEM((len(_NA_VARIANTS), _NA_WIN, LANES), F32)],
        compiler_params=_cparams(1), name="na_attn",
    )(q, k, v, kc, vc, rev)


def _outproj_kernel(om_ref, on_ref, od_ref, sg_ref, w_ref, x_ref, gate_ref, g_ref, o_ref):
    o = jnp.concatenate([om_ref[...], on_ref[...], od_ref[...]], axis=1)
    y = _dot((o.astype(F32) * sg_ref[...].astype(F32)).astype(BF16), w_ref[...])
    o_ref[...] = x_ref[...] + gate_ref[...] * _rms(y, g_ref[...])


def _outproj(om, on, od, sg, w, x, gate, g, bm):
    T, D = x.shape
    row = lambda c: pl.BlockSpec((bm, c), lambda i: (i, 0))
    vec = pl.BlockSpec((1, D), lambda i: (0, 0))
    return pl.pallas_call(
        _outproj_kernel,
        out_shape=jax.ShapeDtypeStruct((T, D), F32),
        grid=(T // bm,),
        in_specs=[row(MLA_W), row(NA_W), row(DIFF_W), row(MIX), pl.BlockSpec(w.shape, lambda i: (0, 0)),
                  row(D), vec, vec],
        out_specs=row(D),
        compiler_params=_cparams(1), name="out_proj",
    )(om, on, od, sg, w, x, gate, g)


def _rot_cols(w):
    q = ROT_DIM // 4
    return jnp.concatenate([-w[..., q:2 * q], w[..., :q], -w[..., 3 * q:], w[..., 2 * q:3 * q]], axis=-1)


def _rope_tables(n):
    t = jnp.arange(n)
    row = (t // GRID_W).astype(F32)
    col = (t % GRID_W).astype(F32)
    half = ROT_DIM // 2
    inv = 1.0 / (ROPE_THETA ** (jnp.arange(0, half, 2, dtype=F32) / half))
    ang_r = row[:, None] * inv[None]
    ang_c = col[:, None] * inv[None]
    ang = jnp.concatenate([ang_r, ang_r, ang_c, ang_c], axis=-1)
    cos, sin = jnp.cos(ang), jnp.sin(ang)
    return jnp.concatenate([cos, cos, sin, sin], axis=-1)


def _prep_weights(w_in, w_uq, w_ukv, w_out):
    D = w_in.shape[0]
    sizes = (MLA_Q_RANK, MLA_KV_RANK, MLA_ROPE, NA_W, NA_W, NA_W, DIFF_W, DIFF_W, DIFF_W, MIX)
    offs = [0]
    for s in sizes:
        offs.append(offs[-1] + s)
    part = lambda i: w_in[:, offs[i]:offs[i + 1]]
    z64 = jnp.zeros((D, LANES - MLA_ROPE), w_in.dtype)
    w_raw = jnp.concatenate([part(0), part(1), part(2), z64, _rot_cols(part(2)), z64], axis=1)
    w_na = jnp.concatenate([part(3), part(4), part(5)], axis=1)
    grp = lambda w: w.reshape(D, -1, ROT_DIM)
    w_dqk = jnp.concatenate([part(6), _rot_cols(grp(part(6))).reshape(D, -1),
                             part(7), _rot_cols(grp(part(7))).reshape(D, -1)], axis=1)
    wq = w_uq.reshape(MLA_Q_RANK, MLA_HEADS, MLA_NOPE + MLA_ROPE)
    zq = jnp.zeros((MLA_Q_RANK, MLA_HEADS, LANES - MLA_ROPE), w_uq.dtype)
    wq_a = jnp.concatenate([wq, zq], axis=-1).reshape(MLA_Q_RANK, -1)
    wq_r = jnp.concatenate([_rot_cols(wq[..., MLA_NOPE:]), zq], axis=-1).reshape(MLA_Q_RANK, -1)
    wkv = w_ukv.reshape(MLA_KV_RANK, MLA_HEADS, MLA_NOPE + MLA_V)
    wkv2 = jnp.concatenate([wkv[..., :MLA_NOPE].reshape(MLA_KV_RANK, -1),
                            wkv[..., MLA_NOPE:].reshape(MLA_KV_RANK, -1)], axis=1)
    c = lambda a: a.astype(BF16)
    return dict(raw=c(w_raw), na=c(w_na), dqk=c(w_dqk), dv=c(part(8)), g=c(part(9)),
                wq=c(jnp.concatenate([wq_a, wq_r], axis=1)), wkv=c(wkv2), out=c(w_out))


def _project(x, pre_g, scale, shift, cs, W, qn, kvn, bm):
    h = _prologue(x, pre_g, scale, shift, bm)
    raw = _row_proj(_proj_raw_kernel, h, W["raw"], (), ((W["raw"].shape[1], F32, False),), bm, "proj_raw")
    nq, nk, nv = _row_proj(functools.partial(_proj_na_kernel, qscale=NA_DIM ** -0.5 * LOG2E), h, W["na"], (),
                           ((NA_W, BF16, False),) * 3, bm, "proj_na")
    dq, dk = _row_proj(functools.partial(_proj_dqk_kernel, qscale=DIFF_QK ** -0.5 * LOG2E), h, W["dqk"], (cs,),
                       ((DIFF_W, BF16, False),) * 2, bm, "proj_dqk")
    dvt = _row_proj(_proj_vt_kernel, h, W["dv"], (), ((DIFF_W, BF16, True),), bm, "proj_dvt")
    sg = _row_proj(_proj_gate_kernel, h, W["g"], (), ((MIX, BF16, False),), bm, "proj_gate")
    qm, km, vmt = _mla_proj(raw, cs, qn, kvn, W["wq"], W["wkv"], bm, (MLA_NOPE + MLA_ROPE) ** -0.5 * LOG2E)
    return dict(nq=nq, nk=nk, nv=nv, dq=dq, dk=dk, dvt=dvt, sg=sg, qm=qm, km=km, vmt=vmt)


def kernel(x, c, ctx, c_ctx, w_ada, b_ada, norm_pre, norm_post, w_in, mla_q_norm, mla_kv_norm, w_uq, w_ukv, na_rpb, diff_lq1, diff_lk1, diff_lq2, diff_lk2, diff_subln, w_out):
    B, n, D = x.shape
    C = ctx.shape[1]
    L = w_in.shape[0]
    assert B == 1 and c.shape[0] == 1
    bm = min(1024, n)
    bq = min(512, n)
    bmc = C

    cc = jnp.zeros((8, D), F32).at[0].set(c[0]).at[1].set(c_ctx)
    mod = _modulation(cc, w_ada, b_ada)
    cs_lat = _rope_tables(n)
    cs_ctx = jnp.concatenate([jnp.ones((C, LANES), F32), jnp.zeros((C, LANES), F32)], axis=1)
    rev = jnp.pad(na_rpb[..., ::-1], ((0, 0), (0, 0), (0, 1), (0, LANES - (2 * WIN_W - 1))))

    xl, xc = x[0], ctx[0]
    for i in range(L):
        last = i == L - 1
        lam_init = 0.8 - 0.6 * math.exp(-0.3 * i)
        W = _prep_weights(w_in[i], w_uq[i], w_ukv[i], w_out[i])
        vecs = lambda r: (mod[i, r:r + 1, :D], mod[i, r:r + 1, D:2 * D], mod[i, r:r + 1, 2 * D:])
        (shift, scale, gate), (shift_c, scale_c, gate_c) = vecs(0), vecs(1)
        pre_g, post_g = norm_pre[i][None], norm_post[i][None]
        qn, kvn = mla_q_norm[i][None], mla_kv_norm[i][None]
        lam_vec = jnp.stack([diff_lq1[i], diff_lk1[i], diff_lq2[i], diff_lk2[i]])
        subln = diff_subln[i][:, None]

        P = _project(xl, pre_g, scale, shift, cs_lat, W, qn, kvn, bm)
        Pc = _project(xc, pre_g, scale_c, shift_c, cs_ctx, W, qn, kvn, bmc)

        o_mla = _flash(P["qm"], Pc["km"], Pc["vmt"], P["km"], P["vmt"], MLA_HEADS, 2 * LANES, MLA_V, bq)
        o_na = _na(P["nq"], P["nk"], P["nv"], Pc["nk"], Pc["nv"], rev[i])
        o_diff = _diff(P["dq"], Pc["dk"], Pc["dvt"], P["dk"], P["dvt"], lam_vec, subln, lam_init, bq)
        new_xl = _outproj(o_mla, o_na, o_diff, P["sg"], W["out"], xl, gate, post_g, min(256, n))

        if not last:
            oc_mla = _flash(Pc["qm"], Pc["km"], Pc["vmt"], None, None, MLA_HEADS, 2 * LANES, MLA_V, C)
            oc_na = _flash(Pc["nq"], Pc["nk"], Pc["nv"].T[None], None, None, NA_HEADS, NA_DIM, NA_DIM, C)
            oc_diff = _diff(Pc["dq"], Pc["dk"], Pc["dvt"], None, None, lam_vec, subln, lam_init, C)
            xc = _outproj(oc_mla, oc_na, oc_diff, Pc["sg"], W["out"], xc, gate_c, post_g, C)
        xl = new_xl
    return xl[None]
```

```python
import functools
import math

import jax
import jax.numpy as jnp
from jax import lax
from jax.experimental import pallas as pl
from jax.experimental.pallas import tpu as pltpu

GRID_W = 64
ROT_DIM = 64
ROPE_THETA = 10000.0
EPS = 1e-6
MLA_HEADS, MLA_NOPE, MLA_ROPE, MLA_V = 6, 128, 64, 128
MLA_Q_RANK, MLA_KV_RANK = 512, 256
NA_HEADS, NA_DIM, WIN_H, WIN_W = 5, 128, 8, 16
DIFF_HEADS, DIFF_QK, DIFF_V = 5, 64, 128
MLA_W = MLA_HEADS * MLA_V
NA_W = NA_HEADS * NA_DIM
DIFF_W = DIFF_HEADS * DIFF_V
MIX = MLA_W + NA_W + DIFF_W

LANES = 128
LOG2E = 1.4426950408889634
NEG = -1e30
VMEM_LIMIT = 56 * 1024 * 1024

F32 = jnp.float32
BF16 = jnp.bfloat16

_NT = (((1,), (1,)), ((), ()))
_TN = (((0,), (0,)), ((), ()))


def _cparams(n_axes):
    return pltpu.CompilerParams(
        dimension_semantics=("arbitrary",) * n_axes, vmem_limit_bytes=VMEM_LIMIT)


def _dot(a, b):
    return jnp.dot(a, b, preferred_element_type=F32)


def _dot_nt(a, b):
    return lax.dot_general(a, b, _NT, preferred_element_type=F32)


def _rms(x, g):
    return x * lax.rsqrt(jnp.mean(x * x, axis=-1, keepdims=True) + EPS) * g


def _mod_kernel(c_ref, w_ref, b_ref, o_ref):
    c = c_ref[...]
    a = c * (1.0 / (1.0 + jnp.exp(-c)))
    o_ref[0] = _dot(a.astype(BF16), w_ref[0].astype(BF16)) + b_ref[0]


def _modulation(cc, w_ada, b_ada):
    L, D, N = w_ada.shape
    tn = 1536
    return pl.pallas_call(
        _mod_kernel,
        out_shape=jax.ShapeDtypeStruct((L, 8, N), F32),
        grid=(L, N // tn),
        in_specs=[pl.BlockSpec((8, D), lambda l, j: (0, 0)),
                  pl.BlockSpec((1, D, tn), lambda l, j: (l, 0, j)),
                  pl.BlockSpec((1, 1, tn), lambda l, j: (l, 0, j))],
        out_specs=pl.BlockSpec((1, 8, tn), lambda l, j: (l, 0, j)),
        compiler_params=_cparams(2), name="adaln_mod",
    )(cc, w_ada, b_ada.reshape(L, 1, N))


def _prologue_kernel(x_ref, g_ref, sc_ref, sh_ref, h_ref):
    y = _rms(x_ref[...], g_ref[...])
    h_ref[...] = (y * (1.0 + sc_ref[...]) + sh_ref[...]).astype(BF16)


def _prologue(x, g, sc, sh, bm):
    T, D = x.shape
    vec = pl.BlockSpec((1, D), lambda i: (0, 0))
    return pl.pallas_call(
        _prologue_kernel,
        out_shape=jax.ShapeDtypeStruct((T, D), BF16),
        grid=(T // bm,),
        in_specs=[pl.BlockSpec((bm, D), lambda i: (i, 0)), vec, vec, vec],
        out_specs=pl.BlockSpec((bm, D), lambda i: (i, 0)),
        compiler_params=_cparams(1), name="prenorm_mod",
    )(x, g, sc, sh)


def _proj_raw_kernel(h_ref, w_ref, o_ref):
    o_ref[...] = _dot(h_ref[...], w_ref[...])


def _proj_na_kernel(h_ref, w_ref, q_ref, k_ref, v_ref, *, qscale):
    r = _dot(h_ref[...], w_ref[...])
    q_ref[...] = (r[:, :NA_W] * qscale).astype(BF16)
    k_ref[...] = r[:, NA_W:2 * NA_W].astype(BF16)
    v_ref[...] = r[:, 2 * NA_W:].astype(BF16)


def _proj_dqk_kernel(h_ref, w_ref, cs_ref, q_ref, k_ref, *, qscale):
    r = _dot(h_ref[...], w_ref[...])
    cos = jnp.concatenate([cs_ref[:, :LANES]] * DIFF_HEADS, axis=1)
    sin = jnp.concatenate([cs_ref[:, LANES:]] * DIFF_HEADS, axis=1)
    W = DIFF_W
    q_ref[...] = ((r[:, :W] * cos + r[:, W:2 * W] * sin) * qscale).astype(BF16)
    k_ref[...] = (r[:, 2 * W:3 * W] * cos + r[:, 3 * W:] * sin).astype(BF16)


def _proj_vt_kernel(h_ref, w_ref, vt_ref):
    vt_ref[0] = _dot(h_ref[...], w_ref[...]).T.astype(BF16)


def _proj_gate_kernel(h_ref, w_ref, o_ref):
    g = _dot(h_ref[...], w_ref[...])
    o_ref[...] = (g * (1.0 / (1.0 + jnp.exp(-g)))).astype(BF16)


def _row_proj(kernel, h, w, extra, outs, bm, name):
    T, D = h.shape
    N = w.shape[1]
    in_specs = [pl.BlockSpec((bm, D), lambda i: (i, 0)), pl.BlockSpec((D, N), lambda i: (0, 0))]
    in_specs += [pl.BlockSpec((bm, e.shape[1]), lambda i: (i, 0)) for e in extra]
    out_shape, out_specs = [], []
    for cols, dt, transposed in outs:
        if transposed:
            out_shape.append(jax.ShapeDtypeStruct((T // bm, cols, bm), dt))
            out_specs.append(pl.BlockSpec((1, cols, bm), lambda i: (i, 0, 0)))
        else:
            out_shape.append(jax.ShapeDtypeStruct((T, cols), dt))
            out_specs.append(pl.BlockSpec((bm, cols), lambda i: (i, 0)))
    single = len(outs) == 1
    return pl.pallas_call(
        kernel,
        out_shape=out_shape[0] if single else tuple(out_shape),
        grid=(T // bm,),
        in_specs=in_specs,
        out_specs=out_specs[0] if single else tuple(out_specs),
        compiler_params=_cparams(1), name=name,
    )(h, w, *extra)


def _mla_proj_kernel(raw_ref, cs_ref, qn_ref, kvn_ref, wq_ref, wkv_ref, qm_ref, km_ref, vt_ref, *, qscale):
    cos = cs_ref[:, :LANES]
    sin = cs_ref[:, LANES:]
    cq = raw_ref[:, :MLA_Q_RANK]
    ckv = raw_ref[:, MLA_Q_RANK:MLA_Q_RANK + MLA_KV_RANK]
    o = MLA_Q_RANK + MLA_KV_RANK
    kr = raw_ref[:, o:o + LANES]
    kr_rot = raw_ref[:, o + LANES:o + 2 * LANES]
    qa = _dot(_rms(cq, qn_ref[...]).astype(BF16), wq_ref[...])
    hw = 2 * LANES
    for h in range(MLA_HEADS):
        rope = qa[:, h * hw + LANES:(h + 1) * hw] * cos + qa[:, MLA_HEADS * hw + h * LANES:MLA_HEADS * hw + (h + 1) * LANES] * sin
        qm_ref[:, h * hw:h * hw + LANES] = (qa[:, h * hw:h * hw + LANES] * qscale).astype(BF16)
        qm_ref[:, h * hw + LANES:(h + 1) * hw] = (rope * qscale).astype(BF16)
    kva = _dot(_rms(ckv, kvn_ref[...]).astype(BF16), wkv_ref[...])
    k_rope = (kr * cos + kr_rot * sin).astype(BF16)
    for h in range(MLA_HEADS):
        km_ref[:, h * hw:h * hw + LANES] = kva[:, h * LANES:(h + 1) * LANES].astype(BF16)
        km_ref[:, h * hw + LANES:(h + 1) * hw] = k_rope
    vt_ref[0] = kva[:, MLA_HEADS * LANES:].T.astype(BF16)


def _mla_proj(raw, cs, qn, kvn, wq, wkv, bm, qscale):
    T = raw.shape[0]
    row = lambda c: pl.BlockSpec((bm, c), lambda i: (i, 0))
    full = lambda a: pl.BlockSpec(a.shape, lambda i: (0, 0))
    hw = 2 * LANES
    return pl.pallas_call(
        functools.partial(_mla_proj_kernel, qscale=qscale),
        out_shape=(jax.ShapeDtypeStruct((T, MLA_HEADS * hw), BF16),
                   jax.ShapeDtypeStruct((T, MLA_HEADS * hw), BF16),
                   jax.ShapeDtypeStruct((T // bm, MLA_W, bm), BF16)),
        grid=(T // bm,),
        in_specs=[row(raw.shape[1]), row(cs.shape[1]), full(qn), full(kvn), full(wq), full(wkv)],
        out_specs=(row(MLA_HEADS * hw), row(MLA_HEADS * hw),
                   pl.BlockSpec((1, MLA_W, bm), lambda i: (i, 0, 0))),
        compiler_params=_cparams(1), name="mla_proj",
    )(raw, cs, qn, kvn, wq, wkv)


def _softmax_first(s):
    m = jnp.max(s, axis=0, keepdims=True)
    p = jnp.exp2(s - m)
    return m, jnp.sum(p, axis=0, keepdims=True), p


def _softmax_next(s, m_prev):
    m = jnp.maximum(m_prev, jnp.max(s, axis=0, keepdims=True))
    p = jnp.exp2(s - m)
    return m, jnp.exp2(m_prev - m), jnp.sum(p, axis=0, keepdims=True), p


def _flash_core(qs, kc, vtc, kl_ref, vtl_ref, accs, ms, ls, sbufs, n_chunks, bkc):
    for a, q in enumerate(qs):
        m, l, p = _softmax_first(_dot_nt(kc, q))
        ms[a][...] = m
        ls[a][...] = l
        accs[a][...] = _dot(vtc, p.astype(BF16))
    if not n_chunks:
        return
    assert n_chunks % 2 == 0

    def scores(c, slot):
        k = kl_ref[pl.ds(pl.multiple_of(c * bkc, bkc), bkc), :]
        for a, q in enumerate(qs):
            sbufs[a][slot][...] = _dot_nt(k, q)

    def consume(c, slot):
        vt = vtl_ref[c]
        for a in range(len(qs)):
            m, alpha, ps, p = _softmax_next(sbufs[a][slot][...], ms[a][...])
            ms[a][...] = m
            ls[a][...] = alpha * ls[a][...] + ps
            accs[a][...] = alpha * accs[a][...] + _dot(vt, p.astype(BF16))

    scores(0, 0)

    def body(i, carry):
        c = 2 * i
        scores(c + 1, 1)
        consume(c, 0)
        scores(c + 2, 0)
        consume(c + 1, 1)
        return carry

    lax.fori_loop(0, n_chunks // 2 - 1, body, 0)
    scores(n_chunks - 1, 1)
    consume(n_chunks - 2, 0)
    consume(n_chunks - 1, 1)


def _flash_specs(q, kc, vtc, kl, vtl, dk, dv, bq):
    C = kc.shape[0]
    in_specs = [pl.BlockSpec((bq, dk), lambda h, i: (i, h)),
                pl.BlockSpec((C, dk), lambda h, i: (0, h)),
                pl.BlockSpec((1, dv, C), lambda h, i: (0, h, 0))]
    args = [q, kc, vtc]
    n_chunks, bkc = 0, 0
    if kl is not None:
        n_chunks, _, bkc = vtl.shape
        in_specs += [pl.BlockSpec((kl.shape[0], dk), lambda h, i: (0, h)),
                     pl.BlockSpec((n_chunks, dv, bkc), lambda h, i: (0, h, 0))]
        args += [kl, vtl]
    return in_specs, args, n_chunks, bkc


def _flash_kernel(q_ref, kc_ref, vtc_ref, *rest, n_chunks, bkc):
    kl_ref, vtl_ref = rest[:2] if n_chunks else (None, None)
    o_ref, acc_ref, m_ref, l_ref = rest[2:6] if n_chunks else rest[:4]
    sbufs = (rest[6:8],) if n_chunks else None
    _flash_core((q_ref[...],), kc_ref[...], vtc_ref[0], kl_ref, vtl_ref,
                (acc_ref,), (m_ref,), (l_ref,), sbufs, n_chunks, bkc)
    o_ref[...] = (acc_ref[...] * (1.0 / l_ref[...])).T.astype(o_ref.dtype)


def _flash(q, kc, vtc, kl, vtl, heads, dk, dv, bq):
    Tq = q.shape[0]
    in_specs, args, n_chunks, bkc = _flash_specs(q, kc, vtc, kl, vtl, dk, dv, bq)
    scratch = [pltpu.VMEM((dv, bq), F32), pltpu.VMEM((1, bq), F32), pltpu.VMEM((1, bq), F32)]
    scratch += [pltpu.VMEM((bkc, bq), F32)] * (2 if n_chunks else 0)
    return pl.pallas_call(
        functools.partial(_flash_kernel, n_chunks=n_chunks, bkc=bkc),
        out_shape=jax.ShapeDtypeStruct((Tq, heads * dv), BF16),
        grid=(heads, Tq // bq),
        in_specs=in_specs,
        out_specs=pl.BlockSpec((bq, dv), lambda h, i: (i, h)),
        scratch_shapes=scratch,
        compiler_params=_cparams(2), name="flash_attn",
    )(*args)


def _diff_kernel(q_ref, kc_ref, vtc_ref, *rest, n_chunks, bkc, lam_init):
    kl_ref, vtl_ref = rest[:2] if n_chunks else (None, None)
    rest = rest[2:] if n_chunks else rest
    lam_ref, sub_ref, o_ref, acc1, acc2, m1, l1, m2, l2 = rest[:9]
    sbufs = (rest[9:11], rest[11:13]) if n_chunks else None
    q = q_ref[...]
    lane = lax.broadcasted_iota(jnp.int32, q.shape, 1)
    zero = jnp.zeros_like(q)
    qs = (jnp.where(lane < DIFF_QK, q, zero), jnp.where(lane >= DIFF_QK, q, zero))
    _flash_core(qs, kc_ref[...], vtc_ref[0], kl_ref, vtl_ref, (acc1, acc2), (m1, m2), (l1, l2), sbufs, n_chunks, bkc)

    lv = lam_ref[...]
    lam = (jnp.exp(jnp.sum(lv[0:1] * lv[1:2], axis=1, keepdims=True))
           - jnp.exp(jnp.sum(lv[2:3] * lv[3:4], axis=1, keepdims=True)) + lam_init)
    o = acc1[...] * (1.0 / l1[...]) - lam * (acc2[...] * (1.0 / l2[...]))
    o = o * lax.rsqrt(jnp.mean(o * o, axis=0, keepdims=True) + EPS) * sub_ref[...] * (1.0 - lam_init)
    o_ref[...] = o.T.astype(o_ref.dtype)


def _diff(q, kc, vtc, kl, vtl, lam_vec, subln, lam_init, bq):
    Tq = q.shape[0]
    dk, dv = 2 * DIFF_QK, DIFF_V
    in_specs, args, n_chunks, bkc = _flash_specs(q, kc, vtc, kl, vtl, dk, dv, bq)
    in_specs += [pl.BlockSpec((4, DIFF_QK), lambda h, i: (0, 0)), pl.BlockSpec((dv, 1), lambda h, i: (0, 0))]
    args += [lam_vec, subln]
    vec = pltpu.VMEM((1, bq), F32)
    scratch = [pltpu.VMEM((dv, bq), F32), pltpu.VMEM((dv, bq), F32), vec, vec, vec, vec]
    scratch += [pltpu.VMEM((bkc, bq), F32)] * (4 if n_chunks else 0)
    return pl.pallas_call(
        functools.partial(_diff_kernel, n_chunks=n_chunks, bkc=bkc, lam_init=lam_init),
        out_shape=jax.ShapeDtypeStruct((Tq, DIFF_HEADS * dv), BF16),
        grid=(DIFF_HEADS, Tq // bq),
        in_specs=in_specs,
        out_specs=pl.BlockSpec((bq, dv), lambda h, i: (i, h)),
        scratch_shapes=scratch,
        compiler_params=_cparams(2), name="diff_attn",
    )(*args)


_NA_WIN = (WIN_H + 1) * GRID_W
_NA_VARIANTS = ((3, (0, 7), (1, 8)),
                (7, (0, 7), (0, 7)),
                (5, (0, 7), (0, 7)),
                (2, (1, 8), (1, 8)),
                (0, (1, 8), (1, 8)))


def _na_build_bias(rev_ref, bias_ref):
    shp = (GRID_W, LANES)
    kc = lax.broadcasted_iota(jnp.int32, shp, 0)
    lane = lax.broadcasted_iota(jnp.int32, shp, 1)
    left = lane < GRID_W
    c = jnp.where(left, lane, lane - GRID_W)
    cs = jnp.clip(c - WIN_W // 2, 0, GRID_W - WIN_W)
    col_ok = (kc >= cs) & (kc < cs + WIN_W)
    neg = jnp.full(shp, NEG, F32)

    def toeplitz(i, shift):
        row = jnp.broadcast_to(rev_ref[i:i + 1, :], shp)
        return pltpu.roll(row, shift, 1, stride=1, stride_axis=0)

    n_off = 2 * WIN_H - 1
    pair = {}
    for i in range(1, n_off):
        t = jnp.where(left, toeplitz(i, LANES - (WIN_W - 1)), toeplitz(i - 1, GRID_W - (WIN_W - 1)))
        pair[i] = jnp.where(col_ok, t * LOG2E, neg)
    for v, (i0, (llo, lhi), (rlo, rhi)) in enumerate(_NA_VARIANTS):
        for t in range(WIN_H + 1):
            lok, rok = llo <= t <= lhi, rlo <= t <= rhi
            if lok and rok:
                tile = pair[i0 + t]
            elif lok:
                tile = jnp.where(left, pair[i0 + t], neg)
            elif rok:
                tile = jnp.where(left, neg, pair[i0 + t])
            else:
                tile = neg
            bias_ref[v, t * GRID_W:(t + 1) * GRID_W, :] = tile


def _na_pair(q, kwin, vwin, kc, vc, bias):
    s_w = _dot_nt(kwin, q) + bias
    s_c = _dot_nt(kc, q)
    m = jnp.maximum(jnp.max(s_w, axis=0, keepdims=True), jnp.max(s_c, axis=0, keepdims=True))
    p_w = jnp.exp2(s_w - m)
    p_c = jnp.exp2(s_c - m)
    inv = 1.0 / (jnp.sum(p_w, axis=0, keepdims=True) + jnp.sum(p_c, axis=0, keepdims=True))
    o = _dot((p_w * inv).T.astype(BF16), vwin) + _dot((p_c * inv).T.astype(BF16), vc)
    return o


def _na_kernel(q_ref, k_ref, v_ref, kc_ref, vc_ref, rev_ref, o_ref, bias_ref, *, rows):
    _na_build_bias(rev_ref.at[0], bias_ref)
    kc = kc_ref[...]
    vc = vc_ref[...]
    pq = 2 * GRID_W

    def run(q0, w0, bias):
        o = _na_pair(q_ref[pl.ds(q0, pq), :], k_ref[pl.ds(w0, _NA_WIN), :], v_ref[pl.ds(w0, _NA_WIN), :],
                     kc, vc, bias)
        o_ref[pl.ds(q0, pq), :] = o.astype(o_ref.dtype)

    run(0, 0, bias_ref[1])
    run(pq, 0, bias_ref[2])
    run((rows - 4) * GRID_W, (rows - 9) * GRID_W, bias_ref[3])
    run((rows - 2) * GRID_W, (rows - 9) * GRID_W, bias_ref[4])

    def body(p, carry):
        q0 = pl.multiple_of(p * pq, pq)
        w0 = pl.multiple_of(p * pq - (WIN_H // 2) * GRID_W, pq)
        run(q0, w0, bias_ref[0])
        return carry

    lax.fori_loop(2, rows // 2 - 2, body, 0)


def _na(q, k, v, kc, vc, rev):
    n = q.shape[0]
    C = kc.shape[0]
    rows = n // GRID_W
    assert rows >= 2 * WIN_H and rows % 2 == 0
    d = NA_DIM
    tok = lambda t: pl.BlockSpec((t, d), lambda h: (0, h))
    return pl.pallas_call(
        functools.partial(_na_kernel, rows=rows),
        out_shape=jax.ShapeDtypeStruct((n, NA_W), BF16),
        grid=(NA_HEADS,),
        in_specs=[tok(n), tok(n), tok(n), tok(C), tok(C),
                  pl.BlockSpec((1, 16, LANES), lambda h: (h, 0, 0))],
        out_specs=tok(n),
        scratch_shapes=[pltpu.VMEM((len(_NA_VARIANTS), _NA_WIN, LANES), F32)],
        compiler_params=_cparams(1), name="na_attn",
    )(q, k, v, kc, vc, rev)


def _outproj_kernel(om_ref, on_ref, od_ref, sg_ref, w_ref, x_ref, gate_ref, g_ref, o_ref):
    o = jnp.concatenate([om_ref[...], on_ref[...], od_ref[...]], axis=1)
    y = _dot((o.astype(F32) * sg_ref[...].astype(F32)).astype(BF16), w_ref[...])
    o_ref[...] = x_ref[...] + gate_ref[...] * _rms(y, g_ref[...])


def _outproj(om, on, od, sg, w, x, gate, g, bm):
    T, D = x.shape
    row = lambda c: pl.BlockSpec((bm, c), lambda i: (i, 0))
    vec = pl.BlockSpec((1, D), lambda i: (0, 0))
    return pl.pallas_call(
        _outproj_kernel,
        out_shape=jax.ShapeDtypeStruct((T, D), F32),
        grid=(T // bm,),
        in_specs=[row(MLA_W), row(NA_W), row(DIFF_W), row(MIX), pl.BlockSpec(w.shape, lambda i: (0, 0)),
                  row(D), vec, vec],
        out_specs=row(D),
        compiler_params=_cparams(1), name="out_proj",
    )(om, on, od, sg, w, x, gate, g)


def _rot_cols(w):
    q = ROT_DIM // 4
    return jnp.concatenate([-w[..., q:2 * q], w[..., :q], -w[..., 3 * q:], w[..., 2 * q:3 * q]], axis=-1)


def _rope_tables(n):
    t = jnp.arange(n)
    row = (t // GRID_W).astype(F32)
    col = (t % GRID_W).astype(F32)
    half = ROT_DIM // 2
    inv = 1.0 / (ROPE_THETA ** (jnp.arange(0, half, 2, dtype=F32) / half))
    ang_r = row[:, None] * inv[None]
    ang_c = col[:, None] * inv[None]
    ang = jnp.concatenate([ang_r, ang_r, ang_c, ang_c], axis=-1)
    cos, sin = jnp.cos(ang), jnp.sin(ang)
    return jnp.concatenate([cos, cos, sin, sin], axis=-1)


def _prep_weights(w_in, w_uq, w_ukv, w_out):
    D = w_in.shape[0]
    sizes = (MLA_Q_RANK, MLA_KV_RANK, MLA_ROPE, NA_W, NA_W, NA_W, DIFF_W, DIFF_W, DIFF_W, MIX)
    offs = [0]
    for s in sizes:
        offs.append(offs[-1] + s)
    part = lambda i: w_in[:, offs[i]:offs[i + 1]]
    z64 = jnp.zeros((D, LANES - MLA_ROPE), w_in.dtype)
    w_raw = jnp.concatenate([part(0), part(1), part(2), z64, _rot_cols(part(2)), z64], axis=1)
    w_na = jnp.concatenate([part(3), part(4), part(5)], axis=1)
    grp = lambda w: w.reshape(D, -1, ROT_DIM)
    w_dqk = jnp.concatenate([part(6), _rot_cols(grp(part(6))).reshape(D, -1),
                             part(7), _rot_cols(grp(part(7))).reshape(D, -1)], axis=1)
    wq = w_uq.reshape(MLA_Q_RANK, MLA_HEADS, MLA_NOPE + MLA_ROPE)
    zq = jnp.zeros((MLA_Q_RANK, MLA_HEADS, LANES - MLA_ROPE), w_uq.dtype)
    wq_a = jnp.concatenate([wq, zq], axis=-1).reshape(MLA_Q_RANK, -1)
    wq_r = jnp.concatenate([_rot_cols(wq[..., MLA_NOPE:]), zq], axis=-1).reshape(MLA_Q_RANK, -1)
    wkv = w_ukv.reshape(MLA_KV_RANK, MLA_HEADS, MLA_NOPE + MLA_V)
    wkv2 = jnp.concatenate([wkv[..., :MLA_NOPE].reshape(MLA_KV_RANK, -1),
                            wkv[..., MLA_NOPE:].reshape(MLA_KV_RANK, -1)], axis=1)
    c = lambda a: a.astype(BF16)
    return dict(raw=c(w_raw), na=c(w_na), dqk=c(w_dqk), dv=c(part(8)), g=c(part(9)),
                wq=c(jnp.concatenate([wq_a, wq_r], axis=1)), wkv=c(wkv2), out=c(w_out))


def _project(x, pre_g, scale, shift, cs, W, qn, kvn, bm):
    h = _prologue(x, pre_g, scale, shift, bm)
    raw = _row_proj(_proj_raw_kernel, h, W["raw"], (), ((W["raw"].shape[1], F32, False),), bm, "proj_raw")
    nq, nk, nv = _row_proj(functools.partial(_proj_na_kernel, qscale=NA_DIM ** -0.5 * LOG2E), h, W["na"], (),
                           ((NA_W, BF16, False),) * 3, bm, "proj_na")
    dq, dk = _row_proj(functools.partial(_proj_dqk_kernel, qscale=DIFF_QK ** -0.5 * LOG2E), h, W["dqk"], (cs,),
                       ((DIFF_W, BF16, False),) * 2, bm, "proj_dqk")
    dvt = _row_proj(_proj_vt_kernel, h, W["dv"], (), ((DIFF_W, BF16, True),), bm, "proj_dvt")
    sg = _row_proj(_proj_gate_kernel, h, W["g"], (), ((MIX, BF16, False),), bm, "proj_gate")
    qm, km, vmt = _mla_proj(raw, cs, qn, kvn, W["wq"], W["wkv"], bm, (MLA_NOPE + MLA_ROPE) ** -0.5 * LOG2E)
    return dict(nq=nq, nk=nk, nv=nv, dq=dq, dk=dk, dvt=dvt, sg=sg, qm=qm, km=km, vmt=vmt)


def kernel(x, c, ctx, c_ctx, w_ada, b_ada, norm_pre, norm_post, w_in, mla_q_norm, mla_kv_norm, w_uq, w_ukv, na_rpb, diff_lq1, diff_lk1, diff_lq2, diff_lk2, diff_subln, w_out):
    B, n, D = x.shape
    C = ctx.shape[1]
    L = w_in.shape[0]
    assert B == 1 and c.shape[0] == 1
    bm = min(1024, n // 2)
    bq = min(512, n)
    bmc = C

    cc = jnp.zeros((8, D), F32).at[0].set(c[0]).at[1].set(c_ctx)
    mod = _modulation(cc, w_ada, b_ada)
    cs_lat = _rope_tables(n)
    cs_ctx = jnp.concatenate([jnp.ones((C, LANES), F32), jnp.zeros((C, LANES), F32)], axis=1)
    rev = jnp.pad(na_rpb[..., ::-1], ((0, 0), (0, 0), (0, 1), (0, LANES - (2 * WIN_W - 1))))

    xl, xc = x[0], ctx[0]
    for i in range(L):
        last = i == L - 1
        lam_init = 0.8 - 0.6 * math.exp(-0.3 * i)
        W = _prep_weights(w_in[i], w_uq[i], w_ukv[i], w_out[i])
        vecs = lambda r: (mod[i, r:r + 1, :D], mod[i, r:r + 1, D:2 * D], mod[i, r:r + 1, 2 * D:])
        (shift, scale, gate), (shift_c, scale_c, gate_c) = vecs(0), vecs(1)
        pre_g, post_g = norm_pre[i][None], norm_post[i][None]
        qn, kvn = mla_q_norm[i][None], mla_kv_norm[i][None]
        lam_vec = jnp.stack([diff_lq1[i], diff_lk1[i], diff_lq2[i], diff_lk2[i]])
        subln = diff_subln[i][:, None]

        P = _project(xl, pre_g, scale, shift, cs_lat, W, qn, kvn, bm)
        Pc = _project(xc, pre_g, scale_c, shift_c, cs_ctx, W, qn, kvn, bmc)

        o_mla = _flash(P["qm"], Pc["km"], Pc["vmt"], P["km"], P["vmt"], MLA_HEADS, 2 * LANES, MLA_V, bq)
        o_na = _na(P["nq"], P["nk"], P["nv"], Pc["nk"], Pc["nv"], rev[i])
        o_diff = _diff(P["dq"], Pc["dk"], Pc["dvt"], P["dk"], P["dvt"], lam_vec, subln, lam_init, bq)
        new_xl = _outproj(o_mla, o_na, o_diff, P["sg"], W["out"], xl, gate, post_g, min(256, n))

        if not last:
            oc_mla = _flash(Pc["qm"], Pc["km"], Pc["vmt"], None, None, MLA_HEADS, 2 * LANES, MLA_V, C)
            oc_na = _flash(Pc["nq"], Pc["nk"], Pc["nv"].T[None], None, None, NA_HEADS, NA_DIM, NA_DIM, C)
            oc_diff = _diff(Pc["dq"], Pc["dk"], Pc["dvt"], None, None, lam_vec, subln, lam_init, C)
            xc = _outproj(oc_mla, oc_na, oc_diff, Pc["sg"], W["out"], xc, gate_c, post_g, C)
        xl = new_xl
    return xl[None]
```

```python
import functools
import math

import jax
import jax.numpy as jnp
from jax import lax
from jax.experimental import pallas as pl
from jax.experimental.pallas import tpu as pltpu

GRID_W = 64
ROT_DIM = 64
ROPE_THETA = 10000.0
EPS = 1e-6
MLA_HEADS, MLA_NOPE, MLA_ROPE, MLA_V = 6, 128, 64, 128
MLA_Q_RANK, MLA_KV_RANK = 512, 256
NA_HEADS, NA_DIM, WIN_H, WIN_W = 5, 128, 8, 16
DIFF_HEADS, DIFF_QK, DIFF_V = 5, 64, 128
MLA_W = MLA_HEADS * MLA_V
NA_W = NA_HEADS * NA_DIM
DIFF_W = DIFF_HEADS * DIFF_V
MIX = MLA_W + NA_W + DIFF_W

LANES = 128
LOG2E = 1.4426950408889634
NEG = -1e30
VMEM_LIMIT = 56 * 1024 * 1024
FLASH_UNROLL = 2
FLASH_BQ = 1024
NA_UNROLL = 4

F32 = jnp.float32
BF16 = jnp.bfloat16

_NT = (((1,), (1,)), ((), ()))


def _cparams(n_axes):
    return pltpu.CompilerParams(
        dimension_semantics=("arbitrary",) * n_axes, vmem_limit_bytes=VMEM_LIMIT)


def _dot(a, b):
    return jnp.dot(a, b, preferred_element_type=F32)


def _dot_nt(a, b):
    return lax.dot_general(a, b, _NT, preferred_element_type=F32)


def _rms(x, g):
    return x * lax.rsqrt(jnp.mean(x * x, axis=-1, keepdims=True) + EPS) * g


def _mod_kernel(c_ref, w_ref, b_ref, o_ref):
    c = c_ref[...]
    a = c * (1.0 / (1.0 + jnp.exp(-c)))
    o_ref[0] = _dot(a.astype(BF16), w_ref[0].astype(BF16)) + b_ref[0]


def _modulation(cc, w_ada, b_ada):
    L, D, N = w_ada.shape
    tn = 1536
    return pl.pallas_call(
        _mod_kernel,
        out_shape=jax.ShapeDtypeStruct((L, 8, N), F32),
        grid=(L, N // tn),
        in_specs=[pl.BlockSpec((8, D), lambda l, j: (0, 0)),
                  pl.BlockSpec((1, D, tn), lambda l, j: (l, 0, j)),
                  pl.BlockSpec((1, 1, tn), lambda l, j: (l, 0, j))],
        out_specs=pl.BlockSpec((1, 8, tn), lambda l, j: (l, 0, j)),
        compiler_params=_cparams(2), name="adaln_mod",
    )(cc, w_ada, b_ada.reshape(L, 1, N))


def _prologue_kernel(x_ref, g_ref, sc_ref, sh_ref, h_ref):
    y = _rms(x_ref[...], g_ref[...])
    h_ref[...] = (y * (1.0 + sc_ref[...]) + sh_ref[...]).astype(BF16)


def _prologue(x, g, sc, sh, bm):
    T, D = x.shape
    vec = pl.BlockSpec((1, D), lambda i: (0, 0))
    return pl.pallas_call(
        _prologue_kernel,
        out_shape=jax.ShapeDtypeStruct((T, D), BF16),
        grid=(T // bm,),
        in_specs=[pl.BlockSpec((bm, D), lambda i: (i, 0)), vec, vec, vec],
        out_specs=pl.BlockSpec((bm, D), lambda i: (i, 0)),
        compiler_params=_cparams(1), name="prenorm_mod",
    )(x, g, sc, sh)


def _proj_raw_kernel(h_ref, w_ref, o_ref):
    o_ref[...] = _dot(h_ref[...], w_ref[...])


def _proj_na_kernel(h_ref, w_ref, q_ref, k_ref, v_ref, *, qscale):
    r = _dot(h_ref[...], w_ref[...])
    q_ref[...] = (r[:, :NA_W] * qscale).astype(BF16)
    k_ref[...] = r[:, NA_W:2 * NA_W].astype(BF16)
    v_ref[...] = r[:, 2 * NA_W:].astype(BF16)


def _proj_dqk_kernel(h_ref, w_ref, cs_ref, q_ref, k_ref, *, qscale):
    r = _dot(h_ref[...], w_ref[...])
    cos = jnp.concatenate([cs_ref[:, :LANES]] * DIFF_HEADS, axis=1)
    sin = jnp.concatenate([cs_ref[:, LANES:]] * DIFF_HEADS, axis=1)
    W = DIFF_W
    q_ref[...] = ((r[:, :W] * cos + r[:, W:2 * W] * sin) * qscale).astype(BF16)
    k_ref[...] = (r[:, 2 * W:3 * W] * cos + r[:, 3 * W:] * sin).astype(BF16)


def _proj_vt_kernel(h_ref, w_ref, vt_ref):
    vt_ref[0] = _dot(h_ref[...], w_ref[...]).T.astype(BF16)


def _proj_gate_kernel(h_ref, w_ref, o_ref):
    g = _dot(h_ref[...], w_ref[...])
    o_ref[...] = (g * (1.0 / (1.0 + jnp.exp(-g)))).astype(BF16)


def _row_proj(kernel, h, w, extra, outs, bm, name):
    T, D = h.shape
    N = w.shape[1]
    in_specs = [pl.BlockSpec((bm, D), lambda i: (i, 0)), pl.BlockSpec((D, N), lambda i: (0, 0))]
    in_specs += [pl.BlockSpec((bm, e.shape[1]), lambda i: (i, 0)) for e in extra]
    out_shape, out_specs = [], []
    for cols, dt, transposed in outs:
        if transposed:
            out_shape.append(jax.ShapeDtypeStruct((T // bm, cols, bm), dt))
            out_specs.append(pl.BlockSpec((1, cols, bm), lambda i: (i, 0, 0)))
        else:
            out_shape.append(jax.ShapeDtypeStruct((T, cols), dt))
            out_specs.append(pl.BlockSpec((bm, cols), lambda i: (i, 0)))
    single = len(outs) == 1
    return pl.pallas_call(
        kernel,
        out_shape=out_shape[0] if single else tuple(out_shape),
        grid=(T // bm,),
        in_specs=in_specs,
        out_specs=out_specs[0] if single else tuple(out_specs),
        compiler_params=_cparams(1), name=name,
    )(h, w, *extra)


def _mla_proj_kernel(raw_ref, cs_ref, qn_ref, kvn_ref, wq_ref, wkv_ref, qm_ref, km_ref, vt_ref, *, qscale):
    cos = cs_ref[:, :LANES]
    sin = cs_ref[:, LANES:]
    cq = raw_ref[:, :MLA_Q_RANK]
    ckv = raw_ref[:, MLA_Q_RANK:MLA_Q_RANK + MLA_KV_RANK]
    o = MLA_Q_RANK + MLA_KV_RANK
    kr = raw_ref[:, o:o + LANES]
    kr_rot = raw_ref[:, o + LANES:o + 2 * LANES]
    qa = _dot(_rms(cq, qn_ref[...]).astype(BF16), wq_ref[...])
    hw = 2 * LANES
    for h in range(MLA_HEADS):
        rope = qa[:, h * hw + LANES:(h + 1) * hw] * cos + qa[:, MLA_HEADS * hw + h * LANES:MLA_HEADS * hw + (h + 1) * LANES] * sin
        qm_ref[:, h * hw:h * hw + LANES] = (qa[:, h * hw:h * hw + LANES] * qscale).astype(BF16)
        qm_ref[:, h * hw + LANES:(h + 1) * hw] = (rope * qscale).astype(BF16)
    kva = _dot(_rms(ckv, kvn_ref[...]).astype(BF16), wkv_ref[...])
    k_rope = (kr * cos + kr_rot * sin).astype(BF16)
    for h in range(MLA_HEADS):
        km_ref[:, h * hw:h * hw + LANES] = kva[:, h * LANES:(h + 1) * LANES].astype(BF16)
        km_ref[:, h * hw + LANES:(h + 1) * hw] = k_rope
    vt_ref[0] = kva[:, MLA_HEADS * LANES:].T.astype(BF16)


def _mla_proj(raw, cs, qn, kvn, wq, wkv, bm, qscale):
    T = raw.shape[0]
    row = lambda c: pl.BlockSpec((bm, c), lambda i: (i, 0))
    full = lambda a: pl.BlockSpec(a.shape, lambda i: (0, 0))
    hw = 2 * LANES
    return pl.pallas_call(
        functools.partial(_mla_proj_kernel, qscale=qscale),
        out_shape=(jax.ShapeDtypeStruct((T, MLA_HEADS * hw), BF16),
                   jax.ShapeDtypeStruct((T, MLA_HEADS * hw), BF16),
                   jax.ShapeDtypeStruct((T // bm, MLA_W, bm), BF16)),
        grid=(T // bm,),
        in_specs=[row(raw.shape[1]), row(cs.shape[1]), full(qn), full(kvn), full(wq), full(wkv)],
        out_specs=(row(MLA_HEADS * hw), row(MLA_HEADS * hw),
                   pl.BlockSpec((1, MLA_W, bm), lambda i: (i, 0, 0))),
        compiler_params=_cparams(1), name="mla_proj",
    )(raw, cs, qn, kvn, wq, wkv)


def _diff_queries(q):
    lane = lax.broadcasted_iota(jnp.int32, q.shape, 1)
    zero = jnp.zeros_like(q)
    return jnp.where(lane < DIFF_QK, q, zero), jnp.where(lane >= DIFF_QK, q, zero)


def _diff_finish(o1, o2, lam_ref, sub_ref, lam_init):
    lv = lam_ref[...]
    lam = (jnp.exp(jnp.sum(lv[0:1] * lv[1:2], axis=1, keepdims=True))
           - jnp.exp(jnp.sum(lv[2:3] * lv[3:4], axis=1, keepdims=True)) + lam_init)
    o = o1 - lam * o2
    return o * lax.rsqrt(jnp.mean(o * o, axis=0, keepdims=True) + EPS) * sub_ref[...] * (1.0 - lam_init)


def _attend_once(q, k, vt):
    s = _dot_nt(k, q)
    p = jnp.exp2(s - jnp.max(s, axis=0, keepdims=True))
    return _dot(vt, p.astype(BF16)) * (1.0 / jnp.sum(p, axis=0, keepdims=True))


def _ctx_flash_kernel(q_ref, k_ref, vt_ref, o_ref):
    o_ref[...] = _attend_once(q_ref[...], k_ref[...], vt_ref[...]).T.astype(o_ref.dtype)


def _ctx_diff_kernel(q_ref, k_ref, vt_ref, lam_ref, sub_ref, o_ref, *, lam_init):
    q1, q2 = _diff_queries(q_ref[...])
    k, vt = k_ref[...], vt_ref[...]
    o = _diff_finish(_attend_once(q1, k, vt), _attend_once(q2, k, vt), lam_ref, sub_ref, lam_init)
    o_ref[...] = o.T.astype(o_ref.dtype)


def _ctx_attn(q, k, vt, heads, dk, dv, diff=None):
    C = q.shape[0]
    in_specs = [pl.BlockSpec((C, dk), lambda h: (0, h)), pl.BlockSpec((C, dk), lambda h: (0, h)),
                pl.BlockSpec((dv, C), lambda h: (h, 0))]
    args = [q, k, vt]
    kern = _ctx_flash_kernel
    if diff is not None:
        lam_vec, subln, lam_init = diff
        in_specs += [pl.BlockSpec((4, DIFF_QK), lambda h: (0, 0)), pl.BlockSpec((dv, 1), lambda h: (0, 0))]
        args += [lam_vec, subln]
        kern = functools.partial(_ctx_diff_kernel, lam_init=lam_init)
    return pl.pallas_call(
        kern,
        out_shape=jax.ShapeDtypeStruct((C, heads * dv), BF16),
        grid=(heads,),
        in_specs=in_specs,
        out_specs=pl.BlockSpec((C, dv), lambda h: (0, h)),
        compiler_params=_cparams(1), name="ctx_attn",
    )(*args)


def _attn_kernel(q_ref, kc_ref, vtc_ref, kl_ref, vtl_ref, *rest, n_maps, n_chunks, bkc, lam_init):
    if n_maps == 2:
        lam_ref, sub_ref = rest[:2]
        rest = rest[2:]
    o_ref = rest[0]
    rest = rest[1:]
    accs, ms, ls = rest[:n_maps], rest[n_maps:2 * n_maps], rest[2 * n_maps:3 * n_maps]
    rest = rest[3 * n_maps:]
    sbufs = [(rest[2 * a], rest[2 * a + 1]) for a in range(n_maps)]
    cbufs = [(rest[2 * n_maps + 2 * a], rest[2 * n_maps + 2 * a + 1]) for a in range(n_maps)]
    q = q_ref[...]
    qs = _diff_queries(q) if n_maps == 2 else (q,)

    kc, vtc = kc_ref[...], vtc_ref[0]
    for a in range(n_maps):
        s = _dot_nt(kc, qs[a])
        m = jnp.max(s, axis=0, keepdims=True)
        p = jnp.exp2(s - m)
        ms[a][...] = m
        ls[a][...] = jnp.sum(p, axis=0, keepdims=True)
        accs[a][...] = _dot(vtc, p.astype(BF16))

    def scores(c, slot):
        k = kl_ref[pl.ds(pl.multiple_of(c * bkc, bkc), bkc), :]
        for a in range(n_maps):
            s = _dot_nt(k, qs[a])
            sbufs[a][slot][...] = s
            cbufs[a][slot][...] = jnp.max(s, axis=0, keepdims=True)

    def consume(c, slot):
        vt = vtl_ref[c]
        for a in range(n_maps):
            m_prev = ms[a][...]
            m = jnp.maximum(m_prev, cbufs[a][slot][...])
            alpha = jnp.exp2(m_prev - m)
            p = jnp.exp2(sbufs[a][slot][...] - m)
            ms[a][...] = m
            ls[a][...] = alpha * ls[a][...] + jnp.sum(p, axis=0, keepdims=True)
            accs[a][...] = alpha * accs[a][...] + _dot(vt, p.astype(BF16))

    unroll = min(FLASH_UNROLL, n_chunks)
    assert unroll % 2 == 0 and n_chunks % unroll == 0

    def run(c0, last):
        for j in range(unroll):
            if not (last and j == unroll - 1):
                scores(c0 + j + 1, (j + 1) % 2)
            consume(c0 + j, j % 2)

    scores(0, 0)
    n_loop = n_chunks // unroll - 1
    lax.fori_loop(0, n_loop, lambda i, carry: (run(i * unroll, False), carry)[1], 0)
    run(n_loop * unroll, True)

    outs = [accs[a][...] * (1.0 / ls[a][...]) for a in range(n_maps)]
    o = outs[0] if n_maps == 1 else _diff_finish(outs[0], outs[1], lam_ref, sub_ref, lam_init)
    o_ref[...] = o.T.astype(o_ref.dtype)


def _attn(q, kc, vtc, kl, vtl, heads, dk, dv, bq, diff=None):
    n = q.shape[0]
    C = kc.shape[0]
    n_chunks, _, bkc = vtl.shape
    n_maps = 1 if diff is None else 2
    in_specs = [pl.BlockSpec((bq, dk), lambda h, i: (i, h)),
                pl.BlockSpec((C, dk), lambda h, i: (0, h)),
                pl.BlockSpec((1, dv, C), lambda h, i: (0, h, 0)),
                pl.BlockSpec((kl.shape[0], dk), lambda h, i: (0, h)),
                pl.BlockSpec((n_chunks, dv, bkc), lambda h, i: (0, h, 0))]
    args = [q, kc, vtc, kl, vtl]
    lam_init = 0.0
    if diff is not None:
        lam_vec, subln, lam_init = diff
        in_specs += [pl.BlockSpec((4, DIFF_QK), lambda h, i: (0, 0)), pl.BlockSpec((dv, 1), lambda h, i: (0, 0))]
        args += [lam_vec, subln]
    vec = pltpu.VMEM((1, bq), F32)
    scratch = [pltpu.VMEM((dv, bq), F32)] * n_maps + [vec] * (2 * n_maps)
    scratch += [pltpu.VMEM((bkc, bq), F32)] * (2 * n_maps) + [vec] * (2 * n_maps)
    return pl.pallas_call(
        functools.partial(_attn_kernel, n_maps=n_maps, n_chunks=n_chunks, bkc=bkc, lam_init=lam_init),
        out_shape=jax.ShapeDtypeStruct((n, heads * dv), BF16),
        grid=(heads, n // bq),
        in_specs=in_specs,
        out_specs=pl.BlockSpec((bq, dv), lambda h, i: (i, h)),
        scratch_shapes=scratch,
        compiler_params=_cparams(2), name="diff_attn" if diff is not None else "flash_attn",
    )(*args)


_NA_WIN = (WIN_H + 1) * GRID_W
_NA_VARIANTS = ((3, (0, 7), (1, 8)),
                (7, (0, 7), (0, 7)),
                (5, (0, 7), (0, 7)),
                (2, (1, 8), (1, 8)),
                (0, (1, 8), (1, 8)))


def _na_build_bias(rev_ref, bias_ref):
    shp = (GRID_W, LANES)
    kc = lax.broadcasted_iota(jnp.int32, shp, 0)
    lane = lax.broadcasted_iota(jnp.int32, shp, 1)
    left = lane < GRID_W
    c = jnp.where(left, lane, lane - GRID_W)
    cs = jnp.clip(c - WIN_W // 2, 0, GRID_W - WIN_W)
    col_ok = (kc >= cs) & (kc < cs + WIN_W)
    neg = jnp.full(shp, NEG, F32)

    def toeplitz(i, shift):
        row = jnp.broadcast_to(rev_ref[i:i + 1, :], shp)
        return pltpu.roll(row, shift, 1, stride=1, stride_axis=0)

    n_off = 2 * WIN_H - 1
    pair = {}
    for i in range(1, n_off):
        t = jnp.where(left, toeplitz(i, LANES - (WIN_W - 1)), toeplitz(i - 1, GRID_W - (WIN_W - 1)))
        pair[i] = jnp.where(col_ok, t * LOG2E, neg)
    for v, (i0, (llo, lhi), (rlo, rhi)) in enumerate(_NA_VARIANTS):
        for t in range(WIN_H + 1):
            lok, rok = llo <= t <= lhi, rlo <= t <= rhi
            if lok and rok:
                tile = pair[i0 + t]
            elif lok:
                tile = jnp.where(left, pair[i0 + t], neg)
            elif rok:
                tile = jnp.where(left, neg, pair[i0 + t])
            else:
                tile = neg
            bias_ref[v, t * GRID_W:(t + 1) * GRID_W, :] = tile


def _na_pair(q, kwin, vwin, kc, vc, bias):
    s_w = _dot_nt(kwin, q) + bias
    s_c = _dot_nt(kc, q)
    m = jnp.maximum(jnp.max(s_w, axis=0, keepdims=True), jnp.max(s_c, axis=0, keepdims=True))
    p_w = jnp.exp2(s_w - m)
    p_c = jnp.exp2(s_c - m)
    inv = 1.0 / (jnp.sum(p_w, axis=0, keepdims=True) + jnp.sum(p_c, axis=0, keepdims=True))
    o = _dot((p_w * inv).T.astype(BF16), vwin) + _dot((p_c * inv).T.astype(BF16), vc)
    return o


def _na_kernel(q_ref, k_ref, v_ref, kc_ref, vc_ref, rev_ref, o_ref, bias_ref, *, rows):
    _na_build_bias(rev_ref.at[0], bias_ref)
    kc = kc_ref[...]
    vc = vc_ref[...]
    pq = 2 * GRID_W

    def run(q0, w0, bias):
        o = _na_pair(q_ref[pl.ds(q0, pq), :], k_ref[pl.ds(w0, _NA_WIN), :], v_ref[pl.ds(w0, _NA_WIN), :],
                     kc, vc, bias)
        o_ref[pl.ds(q0, pq), :] = o.astype(o_ref.dtype)

    run(0, 0, bias_ref[1])
    run(pq, 0, bias_ref[2])
    run((rows - 4) * GRID_W, (rows - 9) * GRID_W, bias_ref[3])
    run((rows - 2) * GRID_W, (rows - 9) * GRID_W, bias_ref[4])

    def body(i, carry):
        for j in range(NA_UNROLL):
            p = 2 + i * NA_UNROLL + j
            q0 = pl.multiple_of(p * pq, pq)
            w0 = pl.multiple_of(p * pq - (WIN_H // 2) * GRID_W, pq)
            run(q0, w0, bias_ref[0])
        return carry

    n_inner = rows // 2 - 4
    assert n_inner % NA_UNROLL == 0
    lax.fori_loop(0, n_inner // NA_UNROLL, body, 0)


def _na(q, k, v, kc, vc, rev):
    n = q.shape[0]
    C = kc.shape[0]
    rows = n // GRID_W
    assert rows >= 2 * WIN_H and rows % 2 == 0
    d = NA_DIM
    tok = lambda t: pl.BlockSpec((t, d), lambda h: (0, h))
    return pl.pallas_call(
        functools.partial(_na_kernel, rows=rows),
        out_shape=jax.ShapeDtypeStruct((n, NA_W), BF16),
        grid=(NA_HEADS,),
        in_specs=[tok(n), tok(n), tok(n), tok(C), tok(C),
                  pl.BlockSpec((1, 16, LANES), lambda h: (h, 0, 0))],
        out_specs=tok(n),
        scratch_shapes=[pltpu.VMEM((len(_NA_VARIANTS), _NA_WIN, LANES), F32)],
        compiler_params=_cparams(1), name="na_attn",
    )(q, k, v, kc, vc, rev)


def _outproj_kernel(om_ref, on_ref, od_ref, sg_ref, w_ref, x_ref, gate_ref, g_ref, o_ref):
    o = jnp.concatenate([om_ref[...], on_ref[...], od_ref[...]], axis=1)
    y = _dot((o.astype(F32) * sg_ref[...].astype(F32)).astype(BF16), w_ref[...])
    o_ref[...] = x_ref[...] + gate_ref[...] * _rms(y, g_ref[...])


def _outproj(om, on, od, sg, w, x, gate, g, bm):
    T, D = x.shape
    row = lambda c: pl.BlockSpec((bm, c), lambda i: (i, 0))
    vec = pl.BlockSpec((1, D), lambda i: (0, 0))
    return pl.pallas_call(
        _outproj_kernel,
        out_shape=jax.ShapeDtypeStruct((T, D), F32),
        grid=(T // bm,),
        in_specs=[row(MLA_W), row(NA_W), row(DIFF_W), row(MIX), pl.BlockSpec(w.shape, lambda i: (0, 0)),
                  row(D), vec, vec],
        out_specs=row(D),
        compiler_params=_cparams(1), name="out_proj",
    )(om, on, od, sg, w, x, gate, g)


def _rot_cols(w):
    q = ROT_DIM // 4
    return jnp.concatenate([-w[..., q:2 * q], w[..., :q], -w[..., 3 * q:], w[..., 2 * q:3 * q]], axis=-1)


def _rope_tables(n):
    t = jnp.arange(n)
    row = (t // GRID_W).astype(F32)
    col = (t % GRID_W).astype(F32)
    half = ROT_DIM // 2
    inv = 1.0 / (ROPE_THETA ** (jnp.arange(0, half, 2, dtype=F32) / half))
    ang_r = row[:, None] * inv[None]
    ang_c = col[:, None] * inv[None]
    ang = jnp.concatenate([ang_r, ang_r, ang_c, ang_c], axis=-1)
    cos, sin = jnp.cos(ang), jnp.sin(ang)
    return jnp.concatenate([cos, cos, sin, sin], axis=-1)


def _prep_weights(w_in, w_uq, w_ukv, w_out):
    D = w_in.shape[0]
    sizes = (MLA_Q_RANK, MLA_KV_RANK, MLA_ROPE, NA_W, NA_W, NA_W, DIFF_W, DIFF_W, DIFF_W, MIX)
    offs = [0]
    for s in sizes:
        offs.append(offs[-1] + s)
    part = lambda i: w_in[:, offs[i]:offs[i + 1]]
    z64 = jnp.zeros((D, LANES - MLA_ROPE), w_in.dtype)
    w_raw = jnp.concatenate([part(0), part(1), part(2), z64, _rot_cols(part(2)), z64], axis=1)
    w_na = jnp.concatenate([part(3), part(4), part(5)], axis=1)
    grp = lambda w: w.reshape(D, -1, ROT_DIM)
    w_dqk = jnp.concatenate([part(6), _rot_cols(grp(part(6))).reshape(D, -1),
                             part(7), _rot_cols(grp(part(7))).reshape(D, -1)], axis=1)
    wq = w_uq.reshape(MLA_Q_RANK, MLA_HEADS, MLA_NOPE + MLA_ROPE)
    zq = jnp.zeros((MLA_Q_RANK, MLA_HEADS, LANES - MLA_ROPE), w_uq.dtype)
    wq_a = jnp.concatenate([wq, zq], axis=-1).reshape(MLA_Q_RANK, -1)
    wq_r = jnp.concatenate([_rot_cols(wq[..., MLA_NOPE:]), zq], axis=-1).reshape(MLA_Q_RANK, -1)
    wkv = w_ukv.reshape(MLA_KV_RANK, MLA_HEADS, MLA_NOPE + MLA_V)
    wkv2 = jnp.concatenate([wkv[..., :MLA_NOPE].reshape(MLA_KV_RANK, -1),
                            wkv[..., MLA_NOPE:].reshape(MLA_KV_RANK, -1)], axis=1)
    c = lambda a: a.astype(BF16)
    return dict(raw=c(w_raw), na=c(w_na), dqk=c(w_dqk), dv=c(part(8)), g=c(part(9)),
                wq=c(jnp.concatenate([wq_a, wq_r], axis=1)), wkv=c(wkv2), out=c(w_out))


def _project(x, pre_g, scale, shift, cs, W, qn, kvn, bm):
    h = _prologue(x, pre_g, scale, shift, bm)
    raw = _row_proj(_proj_raw_kernel, h, W["raw"], (), ((W["raw"].shape[1], F32, False),), bm, "proj_raw")
    nq, nk, nv = _row_proj(functools.partial(_proj_na_kernel, qscale=NA_DIM ** -0.5 * LOG2E), h, W["na"], (),
                           ((NA_W, BF16, False),) * 3, bm, "proj_na")
    dq, dk = _row_proj(functools.partial(_proj_dqk_kernel, qscale=DIFF_QK ** -0.5 * LOG2E), h, W["dqk"], (cs,),
                       ((DIFF_W, BF16, False),) * 2, bm, "proj_dqk")
    dvt = _row_proj(_proj_vt_kernel, h, W["dv"], (), ((DIFF_W, BF16, True),), bm, "proj_dvt")
    sg = _row_proj(_proj_gate_kernel, h, W["g"], (), ((MIX, BF16, False),), bm, "proj_gate")
    qm, km, vmt = _mla_proj(raw, cs, qn, kvn, W["wq"], W["wkv"], bm, (MLA_NOPE + MLA_ROPE) ** -0.5 * LOG2E)
    return dict(nq=nq, nk=nk, nv=nv, dq=dq, dk=dk, dvt=dvt, sg=sg, qm=qm, km=km, vmt=vmt)


def kernel(x, c, ctx, c_ctx, w_ada, b_ada, norm_pre, norm_post, w_in, mla_q_norm, mla_kv_norm, w_uq, w_ukv, na_rpb, diff_lq1, diff_lk1, diff_lq2, diff_lk2, diff_subln, w_out):
    B, n, D = x.shape
    C = ctx.shape[1]
    L = w_in.shape[0]
    assert B == 1 and c.shape[0] == 1
    bm = min(1024, n // 2)
    bq = min(FLASH_BQ, n)
    bmc = C

    cc = jnp.zeros((8, D), F32).at[0].set(c[0]).at[1].set(c_ctx)
    mod = _modulation(cc, w_ada, b_ada)
    cs_lat = _rope_tables(n)
    cs_ctx = jnp.concatenate([jnp.ones((C, LANES), F32), jnp.zeros((C, LANES), F32)], axis=1)
    rev = jnp.pad(na_rpb[..., ::-1], ((0, 0), (0, 0), (0, 1), (0, LANES - (2 * WIN_W - 1))))

    xl, xc = x[0], ctx[0]
    for i in range(L):
        last = i == L - 1
        lam_init = 0.8 - 0.6 * math.exp(-0.3 * i)
        W = _prep_weights(w_in[i], w_uq[i], w_ukv[i], w_out[i])
        vecs = lambda r: (mod[i, r:r + 1, :D], mod[i, r:r + 1, D:2 * D], mod[i, r:r + 1, 2 * D:])
        (shift, scale, gate), (shift_c, scale_c, gate_c) = vecs(0), vecs(1)
        pre_g, post_g = norm_pre[i][None], norm_post[i][None]
        qn, kvn = mla_q_norm[i][None], mla_kv_norm[i][None]
        lam_vec = jnp.stack([diff_lq1[i], diff_lk1[i], diff_lq2[i], diff_lk2[i]])
        subln = diff_subln[i][:, None]

        P = _project(xl, pre_g, scale, shift, cs_lat, W, qn, kvn, bm)
        Pc = _project(xc, pre_g, scale_c, shift_c, cs_ctx, W, qn, kvn, bmc)

        dpar = (lam_vec, subln, lam_init)
        o_mla = _attn(P["qm"], Pc["km"], Pc["vmt"], P["km"], P["vmt"], MLA_HEADS, 2 * LANES, MLA_V, bq)
        o_na = _na(P["nq"], P["nk"], P["nv"], Pc["nk"], Pc["nv"], rev[i])
        o_diff = _attn(P["dq"], Pc["dk"], Pc["dvt"], P["dk"], P["dvt"], DIFF_HEADS, 2 * DIFF_QK, DIFF_V, bq, dpar)
        new_xl = _outproj(o_mla, o_na, o_diff, P["sg"], W["out"], xl, gate, post_g, 256)

        if not last:
            oc_mla = _ctx_attn(Pc["qm"], Pc["km"], Pc["vmt"][0], MLA_HEADS, 2 * LANES, MLA_V)
            oc_na = _ctx_attn(Pc["nq"], Pc["nk"], Pc["nv"].T, NA_HEADS, NA_DIM, NA_DIM)
            oc_diff = _ctx_attn(Pc["dq"], Pc["dk"], Pc["dvt"][0], DIFF_HEADS, 2 * DIFF_QK, DIFF_V, dpar)
            xc = _outproj(oc_mla, oc_na, oc_diff, Pc["sg"], W["out"], xc, gate_c, post_g, C)
        xl = new_xl
    return xl[None]
```
